```python
import jax, jax.numpy as jnp
from jax import lax
import numpy as np

D_MODEL = 1024
BATCH = 8
SEQ = 2048
DEPTH = 4
DEC_BATCH = 128
DEC_SEQ = 4
PAST_LEN = 16384
PAGE_SIZE = 128

GLA_HEADS = 4
GLA_DK = D_MODEL // 2
GLA_DV = D_MODEL
GLA_HDK = GLA_DK // GLA_HEADS
GLA_HDV = GLA_DV // GLA_HEADS
GLA_GATE_RANK = 16
GLA_TAU = 16.0
GLA_CHUNK = 64
GLA_SPLITS = [GLA_DK, 2 * GLA_DK, 2 * GLA_DK + GLA_DV, 2 * GLA_DK + 2 * GLA_DV]
GLA_IN = 2 * GLA_DK + 2 * GLA_DV + GLA_GATE_RANK
D_RNN = 1280
RG_BLOCKS = 8
RG_BLOCK = D_RNN // RG_BLOCKS
CONV_W = 4
RG_C = 8.0
PEER_HEADS = 8
PEER_NKEYS = 128
PEER_N = PEER_NKEYS * PEER_NKEYS
PEER_DQ = 256
PEER_DHALF = PEER_DQ // 2
PEER_TOPK = 16
PEER_TOKEN_BLOCK = 256
EPS = 1e-6
N_GLA = (DEPTH + 1) // 2
N_RG = DEPTH // 2

kernel_name = 'hybrid_gla_rglru_peer_step'


def rms_norm(x, gain):
    xf = x.astype(jnp.float32)
    y = xf * lax.rsqrt(jnp.mean(xf * xf, axis=-1, keepdims=True) + EPS)
    return (y * gain.astype(jnp.float32)).astype(x.dtype)


def gla_mixer(h, s0, w_in, w_gate2, b_gate, onorm_g, w_out):
    B, L, _ = h.shape
    f32 = jnp.float32
    q, k, v, g, lr = jnp.split(h @ w_in, GLA_SPLITS, axis=-1)
    log_a = jax.nn.log_sigmoid((lr @ w_gate2 + b_gate).astype(f32)) / GLA_TAU

    def heads(t, d):
        return t.astype(f32).reshape(B, L, GLA_HEADS, d).transpose(0, 2, 1, 3)

    q = heads(q, GLA_HDK) * (GLA_HDK ** -0.5)
    k = heads(k, GLA_HDK)
    v = heads(v, GLA_HDV)
    log_a = heads(log_a, GLA_HDK)
    C = min(GLA_CHUNK, L)
    n = -(-L // C)
    pad = n * C - L

    def to_chunks(t):
        t = jnp.pad(t, ((0, 0), (0, 0), (0, pad), (0, 0)))
        return t.reshape(B, GLA_HEADS, n, C, t.shape[-1]).transpose(2, 0, 1, 3, 4)

    causal = jnp.tril(jnp.ones((C, C), dtype=bool))

    def chunk_step(S, blk):
        qi, ki, vi, ai = blk
        b = jnp.cumsum(ai, axis=2)
        qd = qi * jnp.exp(b)
        kd = ki * jnp.exp(-b)
        scores = jnp.where(causal, jnp.einsum('bhtd,bhsd->bhts', qd, kd), 0.0)
        o = jnp.einsum('bhts,bhsv->bhtv', scores, vi) + jnp.einsum('bhtd,bhdv->bhtv', qd, S)
        b_end = b[:, :, -1, :]
        k_end = ki * jnp.exp(b_end[:, :, None, :] - b)
        S = S * jnp.exp(b_end)[..., None] + jnp.einsum('bhsd,bhsv->bhdv', k_end, vi)
        return S, o

    S, o = lax.scan(chunk_step, s0.astype(f32), (to_chunks(q), to_chunks(k), to_chunks(v), to_chunks(log_a)))
    o = o.transpose(1, 2, 0, 3, 4).reshape(B, GLA_HEADS, n * C, GLA_HDV)[:, :, :L]
    o = rms_norm(o, onorm_g)
    o = o.transpose(0, 2, 1, 3).reshape(B, L, GLA_DV) * jax.nn.silu(g.astype(f32))
    return o.astype(h.dtype) @ w_out, S.astype(s0.dtype)


def rglru_mixer(h, start, h0, conv0, w_in, conv_w, conv_b, w_a, b_a, w_x, b_x, lam, w_out):
    B, L, _ = h.shape
    f32 = jnp.float32
    y_br, x_br = jnp.split(h @ w_in, [D_RNN], axis=-1)
    xc = jnp.concatenate([conv0.astype(x_br.dtype), x_br], axis=1)
    xconv = conv_b + sum(xc[:, j:j + L] * conv_w[j] for j in range(CONV_W))
    new_conv = xc[:, L:]
    xf = xconv.astype(f32)
    xb = xf.reshape(B, L, RG_BLOCKS, RG_BLOCK)
    gate_r = jax.nn.sigmoid(jnp.einsum('blni,nio->blno', xb, w_a.astype(f32)).reshape(B, L, D_RNN) + b_a)
    gate_i = jax.nn.sigmoid(jnp.einsum('blni,nio->blno', xb, w_x.astype(f32)).reshape(B, L, D_RNN) + b_x)
    log_a = -RG_C * gate_r * jax.nn.softplus(-lam.astype(f32))
    reset = ((start + jnp.arange(L)) == 0)[None, :, None]
    a = jnp.where(reset, 0.0, jnp.exp(log_a))
    mult = jnp.where(reset, 1.0, jnp.sqrt(-jnp.expm1(2.0 * log_a)))
    bx = mult * gate_i * xf

    def combine(left, right):
        return (left[0] * right[0], right[0] * left[1] + right[1])

    a_cum, b_cum = lax.associative_scan(combine, (a, bx), axis=1)
    hs = a_cum * h0.astype(f32)[:, None, :] + b_cum
    out = (hs * jax.nn.gelu(y_br.astype(f32))).astype(h.dtype) @ w_out
    return out, hs[:, -1].astype(h0.dtype), new_conv.astype(conv0.dtype)


def peer_mixer(h, w_q, sub_keys, u_tab, v_tab):
    B, L, D = h.shape
    T = B * L
    nb = -(-T // PEER_TOKEN_BLOCK)
    xt = jnp.pad(h.reshape(T, D), ((0, nb * PEER_TOKEN_BLOCK - T), (0, 0))).reshape(nb, PEER_TOKEN_BLOCK, D)
    f32 = jnp.float32

    def block(xb):
        tb = xb.shape[0]
        q = (xb @ w_q).astype(f32).reshape(tb, PEER_HEADS, 2, PEER_DHALF)
        s = jnp.einsum('thpd,hpnd->thpn', q, sub_keys.astype(f32))
        s_top, i_top = lax.top_k(s, PEER_TOPK)
        cand = (s_top[:, :, 0, :, None] + s_top[:, :, 1, None, :]).reshape(tb, PEER_HEADS, PEER_TOPK * PEER_TOPK)
        cidx = (i_top[:, :, 0, :, None] * PEER_NKEYS + i_top[:, :, 1, None, :]).reshape(tb, PEER_HEADS, PEER_TOPK * PEER_TOPK)
        best, pos = lax.top_k(cand, PEER_TOPK)
        idx = jnp.take_along_axis(cidx, pos, axis=-1)
        g = jax.nn.softmax(best, axis=-1)
        act = jax.nn.gelu(jnp.einsum('thkd,td->thk', u_tab[idx], xb).astype(f32))
        wgt = (g * act).astype(xb.dtype)
        return jnp.einsum('thk,thkd->td', wgt, v_tab[idx])

    out = lax.map(block, xt).reshape(nb * PEER_TOKEN_BLOCK, D)[:T]
    return out.reshape(B, L, D)


def trunk(x, c, start, gla_s0, rg_h0, rg_conv0, p):
    B = x.shape[0]
    gla_new, rg_h_new, rg_conv_new = [], [], []
    cond = jax.nn.silu(c)
    for i in range(DEPTH):
        mod = (cond @ p['ada_w'][i] + p['ada_b'][i]).astype(x.dtype).reshape(B, 6, D_MODEL)
        sh1, sc1, g1, sh2, sc2, g2 = [mod[:, j, None, :] for j in range(6)]
        hm = rms_norm(x, p['norm_mix_g'][i]) * (1 + sc1) + sh1
        j = i // 2
        if i % 2 == 0:
            mixed, s_new = gla_mixer(hm, gla_s0[j], p['gla_w_in'][j], p['gla_w_gate2'][j], p['gla_b_gate'][j],
                                     p['gla_onorm_g'][j], p['gla_w_out'][j])
            gla_new.append(s_new)
        else:
            mixed, h_new, conv_new = rglru_mixer(hm, start, rg_h0[j], rg_conv0[j], p['rg_w_in'][j], p['rg_conv_w'][j],
                                                 p['rg_conv_b'][j], p['rg_w_a'][j], p['rg_b_a'][j], p['rg_w_x'][j],
                                                 p['rg_b_x'][j], p['rg_lambda'][j], p['rg_w_out'][j])
            rg_h_new.append(h_new)
            rg_conv_new.append(conv_new)
        x = x + g1 * mixed
        hm = rms_norm(x, p['norm_ffn_g'][i]) * (1 + sc2) + sh2
        x = x + g2 * peer_mixer(hm, p['peer_w_q'][i], p['peer_sub_keys'][i], p['peer_u'][i], p['peer_v'][i])
    y = rms_norm(x, p['final_norm_g'])
    return y, jnp.stack(gla_new), jnp.stack(rg_h_new), jnp.stack(rg_conv_new)


def setup_inputs(seed: int = 0) -> dict:
    key = jax.random.key(seed)
    ks = iter(jax.random.split(key, 40))
    f32 = jnp.float32

    def nrm(shape, scale):
        return jax.random.normal(next(ks), shape, f32) * scale

    def gain(shape):
        return 1.0 + nrm(shape, 0.01)

    lam_u = jax.random.uniform(next(ks), (N_RG, D_RNN), f32, 0.9, 0.999)
    lam_p = lam_u ** (1.0 / RG_C)
    rg_lambda = jnp.log(lam_p) - jnp.log1p(-lam_p)
    return {
        'x_prompt': nrm((BATCH, SEQ, D_MODEL), 1.0),
        'x_sample': nrm((DEC_BATCH, DEC_SEQ, D_MODEL), 1.0),
        'c_prompt': nrm((BATCH, D_MODEL), 1.0),
        'c_sample': nrm((DEC_BATCH, D_MODEL), 1.0),
        'state_gla': nrm((N_GLA, DEC_BATCH, GLA_HEADS, GLA_HDK, GLA_HDV), 1.0),
        'state_rglru_h': nrm((N_RG, DEC_BATCH, D_RNN), 0.5),
        'state_rglru_conv': nrm((N_RG, DEC_BATCH, CONV_W - 1, D_RNN), 1.0),
        'ada_w': nrm((DEPTH, D_MODEL, 6 * D_MODEL), 0.5 * D_MODEL ** -0.5),
        'ada_b': nrm((DEPTH, 6 * D_MODEL), 0.01),
        'norm_mix_g': gain((DEPTH, D_MODEL)),
        'norm_ffn_g': gain((DEPTH, D_MODEL)),
        'gla_w_in': nrm((N_GLA, D_MODEL, GLA_IN), D_MODEL ** -0.5),
        'gla_w_gate2': nrm((N_GLA, GLA_GATE_RANK, GLA_DK), GLA_GATE_RANK ** -0.5),
        'gla_b_gate': nrm((N_GLA, GLA_DK), 0.01),
        'gla_onorm_g': gain((N_GLA, GLA_HDV)),
        'gla_w_out': nrm((N_GLA, GLA_DV, D_MODEL), GLA_DV ** -0.5),
        'rg_w_in': nrm((N_RG, D_MODEL, 2 * D_RNN), D_MODEL ** -0.5),
        'rg_conv_w': nrm((N_RG, CONV_W, D_RNN), CONV_W ** -0.5),
        'rg_conv_b': nrm((N_RG, D_RNN), 0.01),
        'rg_w_a': nrm((N_RG, RG_BLOCKS, RG_BLOCK, RG_BLOCK), RG_BLOCK ** -0.5),
        'rg_b_a': nrm((N_RG, D_RNN), 0.01),
        'rg_w_x': nrm((N_RG, RG_BLOCKS, RG_BLOCK, RG_BLOCK), RG_BLOCK ** -0.5),
        'rg_b_x': nrm((N_RG, D_RNN), 0.01),
        'rg_lambda': rg_lambda,
        'rg_w_out': nrm((N_RG, D_RNN, D_MODEL), D_RNN ** -0.5),
        'peer_w_q': nrm((DEPTH, D_MODEL, PEER_HEADS * PEER_DQ), D_MODEL ** -0.5),
        'peer_sub_keys': nrm((DEPTH, PEER_HEADS, 2, PEER_NKEYS, PEER_DHALF), PEER_DHALF ** -0.5),
        'peer_u': nrm((DEPTH, PEER_N, D_MODEL), D_MODEL ** -0.5),
        'peer_v': nrm((DEPTH, PEER_N, D_MODEL), 0.5),
        'final_norm_g': gain((D_MODEL,)),
    }


def reference(x_prompt, x_sample, c_prompt, c_sample, state_gla, state_rglru_h, state_rglru_conv,
              ada_w, ada_b, norm_mix_g, norm_ffn_g, gla_w_in, gla_w_gate2, gla_b_gate, gla_onorm_g, gla_w_out,
              rg_w_in, rg_conv_w, rg_conv_b, rg_w_a, rg_b_a, rg_w_x, rg_b_x, rg_lambda, rg_w_out,
              peer_w_q, peer_sub_keys, peer_u, peer_v, final_norm_g):
    p = dict(ada_w=ada_w, ada_b=ada_b, norm_mix_g=norm_mix_g, norm_ffn_g=norm_ffn_g,
             gla_w_in=gla_w_in, gla_w_gate2=gla_w_gate2, gla_b_gate=gla_b_gate, gla_onorm_g=gla_onorm_g,
             gla_w_out=gla_w_out, rg_w_in=rg_w_in, rg_conv_w=rg_conv_w, rg_conv_b=rg_conv_b, rg_w_a=rg_w_a,
             rg_b_a=rg_b_a, rg_w_x=rg_w_x, rg_b_x=rg_b_x, rg_lambda=rg_lambda, rg_w_out=rg_w_out,
             peer_w_q=peer_w_q, peer_sub_keys=peer_sub_keys, peer_u=peer_u, peer_v=peer_v,
             final_norm_g=final_norm_g)
    dt = x_prompt.dtype
    gla0 = jnp.zeros((N_GLA, BATCH, GLA_HEADS, GLA_HDK, GLA_HDV), state_gla.dtype)
    h0 = jnp.zeros((N_RG, BATCH, D_RNN), state_rglru_h.dtype)
    conv0 = jnp.zeros((N_RG, BATCH, CONV_W - 1, D_RNN), state_rglru_conv.dtype)
    y_prompt, gla_p, h_p, conv_p = trunk(x_prompt, c_prompt.astype(dt), 0, gla0, h0, conv0, p)
    y_sample, gla_s, h_s, conv_s = trunk(x_sample, c_sample.astype(x_sample.dtype), PAST_LEN,
                                         state_gla, state_rglru_h, state_rglru_conv, p)
    return (y_prompt, y_sample, gla_p, gla_s, h_p, h_s, conv_p, conv_s)
```

```python
import functools

import jax
import jax.numpy as jnp
from jax import lax
from jax.experimental import pallas as pl
from jax.experimental.pallas import tpu as pltpu
from jax.experimental.pallas import tpu_sc as plsc

f32 = jnp.float32
bf16 = jnp.bfloat16
HIGHEST = lax.Precision.HIGHEST

D_MODEL = 1024
DEPTH = 4
GLA_HEADS = 4
GLA_DK = 512
GLA_DV = 1024
GLA_HDK = 128
GLA_HDV = 256
GLA_RANK = 16
GLA_TAU = 16.0
GLA_CHUNK = 64
D_RNN = 1280
RG_BLOCKS = 8
RG_BLOCK = 160
CONV_W = 4
RG_C = 8.0
PEER_HEADS = 8
PEER_NKEYS = 128
PEER_DHALF = 128
PEER_TOPK = 16
PEER_ROWS = PEER_HEADS * PEER_TOPK
PAST_LEN = 16384
EPS = 1e-6

SC_CORES = 2
SC_SUBCORES = 16
SC_LANES = 16
SC_WORKERS = SC_CORES * SC_SUBCORES
LANES = 128
HALF_W = D_MODEL // 2
SC_CHUNK_ROWS = 64
SC_TOK_BLOCK = 8


def _rms(x, gain):
    return x * lax.rsqrt(jnp.mean(x * x, axis=-1, keepdims=True) + EPS) * gain


def _adaln_kernel(c_ref, w_ref, b_ref, o_ref):
    cond = jax.nn.silu(c_ref[...])
    o_ref[0] = jnp.dot(cond, w_ref[0], precision=HIGHEST, preferred_element_type=f32) + b_ref[0]


def adaln(c, ada_w, ada_b):
    bc = c.shape[0]
    tn = 1024
    return pl.pallas_call(
        _adaln_kernel,
        grid=(DEPTH, 6 * D_MODEL // tn),
        in_specs=[
            pl.BlockSpec((bc, D_MODEL), lambda i, j: (0, 0)),
            pl.BlockSpec((1, D_MODEL, tn), lambda i, j: (i, 0, j)),
            pl.BlockSpec((1, 1, tn), lambda i, j: (i, 0, j)),
        ],
        out_specs=pl.BlockSpec((1, bc, tn), lambda i, j: (i, 0, j)),
        out_shape=jax.ShapeDtypeStruct((DEPTH, bc, 6 * D_MODEL), f32),
        name="adaln",
    )(c, ada_w, ada_b.reshape(DEPTH, 1, 6 * D_MODEL))


def _proj_kernel(*refs, has_delta, has_w, want_h):
    it = iter(refs)
    x_ref = next(it)
    delta_ref = next(it) if has_delta else None
    gate_ref = next(it) if has_delta else None
    g_ref, sc_ref, sh_ref = next(it), next(it), next(it)
    w_ref = next(it) if has_w else None
    xo_ref = next(it) if has_delta else None
    h_ref = next(it) if want_h else None
    y_ref = next(it) if has_w else None
    hb_ref = next(it) if has_w else None

    def prologue():
        x = x_ref[...]
        if has_delta:
            x = x + gate_ref[...].reshape(-1, D_MODEL) * delta_ref[...]
            xo_ref[...] = x
        h = _rms(x, g_ref[...])
        h = h * (1.0 + sc_ref[...].reshape(-1, D_MODEL)) + sh_ref[...].reshape(-1, D_MODEL)
        if want_h:
            h_ref[...] = h
        if has_w:
            hb_ref[...] = h.astype(bf16)

    if has_w:
        pl.when(pl.program_id(1) == 0)(prologue)
        y_ref[...] = jnp.dot(hb_ref[...], w_ref[...], preferred_element_type=f32)
    else:
        prologue()


def _mod_spec(mod, tm, seq_len):
    if mod.ndim == 3:
        per = seq_len // tm
        return pl.BlockSpec((1, 1, D_MODEL), lambda i, j: (i // per, 0, 0))
    return pl.BlockSpec((tm, D_MODEL), lambda i, j: (i, 0))


def norm_proj(x, gain, scale, shift, w=None, *, delta=None, gate=None, want_h=False, seq_len, tm, tn=512):
    t = x.shape[0]
    has_delta, has_w = delta is not None, w is not None
    n = w.shape[1] if has_w else 0
    tn = min(tn, n) if has_w else 0
    grid = (t // tm, n // tn if has_w else 1)
    row = pl.BlockSpec((tm, D_MODEL), lambda i, j: (i, 0))
    args, specs = [x], [row]
    if has_delta:
        args += [delta, gate]
        specs += [row, _mod_spec(gate, tm, seq_len)]
    args += [gain.reshape(1, D_MODEL), scale, shift]
    specs += [pl.BlockSpec((1, D_MODEL), lambda i, j: (0, 0)), _mod_spec(scale, tm, seq_len),
              _mod_spec(shift, tm, seq_len)]
    out_shapes, out_specs, scratch = [], [], []
    if has_delta:
        out_shapes.append(jax.ShapeDtypeStruct((t, D_MODEL), f32))
        out_specs.append(row)
    if want_h:
        out_shapes.append(jax.ShapeDtypeStruct((t, D_MODEL), f32))
        out_specs.append(row)
    if has_w:
        args.append(w)
        specs.append(pl.BlockSpec((D_MODEL, tn), lambda i, j: (0, j)))
        out_shapes.append(jax.ShapeDtypeStruct((t, n), f32))
        out_specs.append(pl.BlockSpec((tm, tn), lambda i, j: (i, j)))
        scratch.append(pltpu.VMEM((tm, D_MODEL), bf16))
    return pl.pallas_call(
        functools.partial(_proj_kernel, has_delta=has_delta, has_w=has_w, want_h=want_h),
        grid=grid, in_specs=specs, out_specs=out_specs, out_shape=out_shapes,
        scratch_shapes=scratch,
        compiler_params=pltpu.CompilerParams(dimension_semantics=("parallel", "arbitrary")),
        name="norm_proj",
    )(*args)


def _out_proj_kernel(a_ref, w_ref, x_ref, gate_ref, o_ref):
    y = jnp.dot(a_ref[...], w_ref[...], preferred_element_type=f32)
    o_ref[...] = x_ref[...] + gate_ref[...].reshape(-1, D_MODEL) * y


def out_proj(a, w, x, gate, *, seq_len, tm):
    t, k = a.shape
    row = pl.BlockSpec((tm, D_MODEL), lambda i, j: (i, 0))
    return pl.pallas_call(
        _out_proj_kernel,
        grid=(t // tm, 1),
        in_specs=[pl.BlockSpec((tm, k), lambda i, j: (i, 0)),
                  pl.BlockSpec((k, D_MODEL), lambda i, j: (0, 0)),
                  row, _mod_spec(gate, tm, seq_len)],
        out_specs=row,
        out_shape=jax.ShapeDtypeStruct((t, D_MODEL), f32),
        name="out_proj",
    )(a, w, x, gate)


def _gla_kernel(q_ref, k_ref, v_ref, g_ref, lr_ref, wg_ref, bg_ref, on_ref, s0_ref,
                o_ref, so_ref, s_scr, *, bb, chunk, valid):
    c = pl.program_id(1)

    @pl.when(c == 0)
    def _():
        s_scr[...] = s0_ref[...]

    rows = lax.broadcasted_iota(jnp.int32, (chunk, chunk), 0)
    cols = lax.broadcasted_iota(jnp.int32, (chunk, chunk), 1)
    causal = rows >= cols
    tri = causal.astype(f32)
    ones = jnp.ones((chunk, GLA_HDV), f32)
    nt = (((1,), (1,)), ((), ()))
    tn = (((0,), (0,)), ((), ()))
    for bi in range(bb):
        gate_in = jnp.dot(lr_ref[bi].astype(bf16), wg_ref[...], preferred_element_type=f32) + bg_ref[...]
        log_a = jax.nn.log_sigmoid(gate_in) / GLA_TAU
        if valid < chunk:
            log_a = jnp.where(lax.broadcasted_iota(jnp.int32, log_a.shape, 0) < valid, log_a, 0.0)
        for h in range(GLA_HEADS):
            ks = slice(h * GLA_HDK, (h + 1) * GLA_HDK)
            vs = slice(h * GLA_HDV, (h + 1) * GLA_HDV)
            la = log_a[:, ks]
            b = jnp.dot(tri, la, precision=HIGHEST, preferred_element_type=f32)
            q = q_ref[bi, :, ks] * (GLA_HDK ** -0.5)
            k = k_ref[bi, :, ks]
            v = v_ref[bi, :, vs].astype(bf16)
            qd = (q * jnp.exp(b)).astype(bf16)
            kd = (k * jnp.exp(-b)).astype(bf16)
            scores = lax.dot_general(qd, kd, nt, preferred_element_type=f32)
            scores = jnp.where(causal, scores, 0.0).astype(bf16)
            s_old = s_scr[bi, h]
            o = (jnp.dot(scores, v, preferred_element_type=f32)
                 + jnp.dot(qd, s_old.astype(bf16), preferred_element_type=f32))
            b_end = b[chunk - 1:chunk, :]
            k_end = (k * jnp.exp(b_end - b)).astype(bf16)
            decay = jnp.exp(lax.dot_general(la, ones, tn, precision=HIGHEST, preferred_element_type=f32))
            s_scr[bi, h] = s_old * decay + lax.dot_general(k_end, v, tn, preferred_element_type=f32)
            on = _rms(o, on_ref[...])
            o_ref[bi, :, vs] = (on * jax.nn.silu(g_ref[bi, :, vs])).astype(o_ref.dtype)

    @pl.when(c == pl.num_programs(1) - 1)
    def _():
        so_ref[...] = s_scr[...]


def gla_mix(y, lr, w_gate2p, b_gate, onorm_g, s0, *, bb, chunk, valid):
    b, seq, _ = y.shape
    grid = (b // bb, seq // chunk)

    def col(width, blk):
        return pl.BlockSpec((bb, chunk, width), lambda i, c: (i, c, blk))

    st = pl.BlockSpec((bb, GLA_HEADS, GLA_HDK, GLA_HDV), lambda i, c: (i, 0, 0, 0))
    return pl.pallas_call(
        functools.partial(_gla_kernel, bb=bb, chunk=chunk, valid=valid),
        grid=grid,
        in_specs=[col(GLA_DK, 0), col(GLA_DK, 1), col(GLA_DV, 1), col(GLA_DV, 2),
                  pl.BlockSpec((bb, chunk, LANES), lambda i, c: (i, c, 0)),
                  pl.BlockSpec((LANES, GLA_DK), lambda i, c: (0, 0)),
                  pl.BlockSpec((1, GLA_DK), lambda i, c: (0, 0)),
                  pl.BlockSpec((1, GLA_HDV), lambda i, c: (0, 0)),
                  st],
        out_specs=[pl.BlockSpec((bb, chunk, GLA_DV), lambda i, c: (i, c, 0)), st],
        out_shape=[jax.ShapeDtypeStruct((b, seq, GLA_DV), bf16),
                   jax.ShapeDtypeStruct(s0.shape, f32)],
        scratch_shapes=[pltpu.VMEM((bb, GLA_HEADS, GLA_HDK, GLA_HDV), f32)],
        compiler_params=pltpu.CompilerParams(dimension_semantics=("parallel", "arbitrary")),
        name="gla_mix",
    )(y, y, y, y, lr, w_gate2p, b_gate.reshape(1, GLA_DK), onorm_g.reshape(1, GLA_HDV), s0)


def _rg_gates(xconv, wa_ref, ba_ref, wx_ref, bx_ref, lam_ref):
    xb = xconv.astype(bf16)
    gate_r = jax.nn.sigmoid(jnp.dot(xb, wa_ref[...], preferred_element_type=f32) + ba_ref[...])
    gate_i = jax.nn.sigmoid(jnp.dot(xb, wx_ref[...], preferred_element_type=f32) + bx_ref[...])
    log_a = -RG_C * gate_r * jax.nn.softplus(-lam_ref[...])
    a = jnp.exp(log_a)
    mult = jnp.sqrt(-jnp.tanh(log_a) * (a * a + 1.0))
    return a, mult, gate_i


def _rg_prefill_kernel(y_ref, xb_ref, h0_ref, c0_ref, cw_ref, cb_ref, wa_ref, ba_ref, wx_ref, bx_ref,
                       lam_ref, o_ref, ho_ref, co_ref, buf, h_scr, *, tl, start):
    t = pl.program_id(1)
    top = 8

    @pl.when(t == 0)
    def _():
        h_scr[...] = h0_ref[0]
        buf[top - (CONV_W - 1):top, :] = c0_ref[0]

    x = xb_ref[0]
    buf[top:top + tl, :] = x
    xconv = cb_ref[...] + sum(buf[top - (CONV_W - 1) + j:top - (CONV_W - 1) + j + tl, :] * cw_ref[j:j + 1, :]
                              for j in range(CONV_W))
    a, mult, gate_i = _rg_gates(xconv, wa_ref, ba_ref, wx_ref, bx_ref, lam_ref)
    row = lax.broadcasted_iota(jnp.int32, (tl, D_RNN), 0)
    reset = (row + t * tl + start) == 0
    a = jnp.where(reset, 0.0, a)
    mult = jnp.where(reset, 1.0, mult)
    b = mult * gate_i * xconv
    s = 1
    while s < tl:
        keep = row >= s
        b = jnp.where(keep, a * pltpu.roll(b, s, 0) + b, b)
        a = jnp.where(keep, a * pltpu.roll(a, s, 0), a)
        s *= 2
    hs = a * h_scr[...] + b
    o_ref[0] = (hs * jax.nn.gelu(y_ref[0])).astype(o_ref.dtype)
    h_scr[...] = hs[tl - 1:tl, :]
    buf[top - (CONV_W - 1):top, :] = x[tl - (CONV_W - 1):tl, :]

    @pl.when(t == pl.num_programs(1) - 1)
    def _():
        ho_ref[0] = hs[tl - 1:tl, :]
        co_ref[0] = x[tl - (CONV_W - 1):tl, :]


def _rg_vec(v):
    return v.reshape(1, D_RNN)


def rglru_prefill(y, h0, conv0, conv_w, conv_b, wa, ba, wx, bx, lam, *, start, tl):
    b, seq, _ = y.shape
    assert seq % tl == 0 and tl >= CONV_W - 1
    vec = pl.BlockSpec((1, D_RNN), lambda i, t: (0, 0))
    mat = pl.BlockSpec((D_RNN, D_RNN), lambda i, t: (0, 0))
    return pl.pallas_call(
        functools.partial(_rg_prefill_kernel, tl=tl, start=start),
        grid=(b, seq // tl),
        in_specs=[pl.BlockSpec((1, tl, D_RNN), lambda i, t: (i, t, 0)),
                  pl.BlockSpec((1, tl, D_RNN), lambda i, t: (i, t, 1)),
                  pl.BlockSpec((1, 1, D_RNN), lambda i, t: (i, 0, 0)),
                  pl.BlockSpec((1, CONV_W - 1, D_RNN), lambda i, t: (i, 0, 0)),
                  pl.BlockSpec((CONV_W, D_RNN), lambda i, t: (0, 0)),
                  vec, mat, vec, mat, vec, vec],
        out_specs=[pl.BlockSpec((1, tl, D_RNN), lambda i, t: (i, t, 0)),
                   pl.BlockSpec((1, 1, D_RNN), lambda i, t: (i, 0, 0)),
                   pl.BlockSpec((1, CONV_W - 1, D_RNN), lambda i, t: (i, 0, 0))],
        out_shape=[jax.ShapeDtypeStruct((b, seq, D_RNN), bf16),
                   jax.ShapeDtypeStruct((b, 1, D_RNN), f32),
                   jax.ShapeDtypeStruct((b, CONV_W - 1, D_RNN), f32)],
        scratch_shapes=[pltpu.VMEM((8 + tl, D_RNN), f32), pltpu.VMEM((1, D_RNN), f32)],
        compiler_params=pltpu.CompilerParams(dimension_semantics=("parallel", "arbitrary")),
        name="rglru_prefill",
    )(y, y, h0.reshape(b, 1, D_RNN), conv0, conv_w, _rg_vec(conv_b), wa, _rg_vec(ba), wx, _rg_vec(bx),
      _rg_vec(lam))


def _rg_decode_kernel(y_ref, xb_ref, h0_ref, c0_ref, cw_ref, cb_ref, wa_ref, ba_ref, wx_ref, bx_ref,
                      lam_ref, o_ref, ho_ref, *, steps, start):
    xc = [c0_ref[j] for j in range(CONV_W - 1)] + [xb_ref[s] for s in range(steps)]
    h = h0_ref[...]
    for s in range(steps):
        xconv = cb_ref[...] + sum(xc[s + j] * cw_ref[j:j + 1, :] for j in range(CONV_W))
        a, mult, gate_i = _rg_gates(xconv, wa_ref, ba_ref, wx_ref, bx_ref, lam_ref)
        if start + s == 0:
            a, mult = 0.0, 1.0
        h = a * h + mult * gate_i * xconv
        o_ref[s] = (h * jax.nn.gelu(y_ref[s])).astype(o_ref.dtype)
    ho_ref[...] = h


def rglru_decode(y, h0, conv0, conv_w, conv_b, wa, ba, wx, bx, lam, *, start):
    steps, b, _ = y.shape
    whole = lambda *shape: pl.BlockSpec(shape, lambda i: (0,) * len(shape))
    vec, mat = whole(1, D_RNN), whole(D_RNN, D_RNN)
    return pl.pallas_call(
        functools.partial(_rg_decode_kernel, steps=steps, start=start),
        grid=(1,),
        in_specs=[pl.BlockSpec((steps, b, D_RNN), lambda i: (0, 0, 0)),
                  pl.BlockSpec((steps, b, D_RNN), lambda i: (0, 0, 1)),
                  whole(b, D_RNN), whole(CONV_W - 1, b, D_RNN), whole(CONV_W, D_RNN),
                  vec, mat, vec, mat, vec, vec],
        out_specs=[whole(steps, b, D_RNN), whole(b, D_RNN)],
        out_shape=[jax.ShapeDtypeStruct((steps, b, D_RNN), bf16),
                   jax.ShapeDtypeStruct((b, D_RNN), f32)],
        name="rglru_decode",
    )(y, y, h0, conv0, conv_w, _rg_vec(conv_b), wa, _rg_vec(ba), wx, _rg_vec(bx), _rg_vec(lam))


def _top_rows(v, nrows, ntok):
    rowi = lax.broadcasted_iota(jnp.int32, (nrows, ntok), 0)
    vals, idxs = [], []
    for _ in range(PEER_TOPK):
        m = jnp.max(v, axis=0, keepdims=True)
        i = jnp.min(jnp.where(v == m, rowi, nrows), axis=0, keepdims=True)
        vals.append(m)
        idxs.append(i)
        v = jnp.where(rowi == i, -jnp.inf, v)
    return jnp.concatenate(vals, axis=0), jnp.concatenate(idxs, axis=0)


def _peer_topk_kernel(q_ref, keys_ref, idx_ref, g_ref, *, tm):
    nt = (((1,), (1,)), ((), ()))
    idx_rows, g_rows = [], []
    for h in range(PEER_HEADS):
        top = []
        for p in range(2):
            col = (h * 2 + p) * PEER_DHALF
            qs = q_ref[:, col:col + PEER_DHALF].astype(bf16)
            s = lax.dot_general(keys_ref[h, p], qs, nt, preferred_element_type=f32)
            top.append(_top_rows(s, PEER_NKEYS, tm))
        (s0, i0), (s1, i1) = top
        cand = jnp.concatenate([s0[a:a + 1] + s1 for a in range(PEER_TOPK)], axis=0)
        cidx = jnp.concatenate([i0[a:a + 1] * PEER_NKEYS + i1 for a in range(PEER_TOPK)], axis=0)
        best, pos = _top_rows(cand, PEER_TOPK * PEER_TOPK, tm)
        rowi = lax.broadcasted_iota(jnp.int32, (PEER_TOPK * PEER_TOPK, tm), 0)
        idx = jnp.concatenate(
            [jnp.max(jnp.where(rowi == pos[k:k + 1], cidx, -1), axis=0, keepdims=True)
             for k in range(PEER_TOPK)], axis=0)
        e = jnp.exp(best - best[0:1])
        g_rows.append(e / jnp.sum(e, axis=0, keepdims=True))
        idx_rows.append(idx.astype(f32))
    idx_ref[...] = jnp.concatenate(idx_rows, axis=0).T.astype(jnp.int32)
    g_ref[...] = jnp.concatenate(g_rows, axis=0).T


def peer_topk(q, keys, *, tm):
    t = q.shape[0]
    out = pl.BlockSpec((tm, PEER_ROWS), lambda i: (i, 0))
    return pl.pallas_call(
        functools.partial(_peer_topk_kernel, tm=tm),
        grid=(t // tm,),
        in_specs=[pl.BlockSpec((tm, 2 * PEER_HEADS * PEER_DHALF), lambda i: (i, 0)),
                  pl.BlockSpec((PEER_HEADS, 2, PEER_NKEYS, PEER_DHALF), lambda i: (0, 0, 0, 0))],
        out_specs=[out, out],
        out_shape=[jax.ShapeDtypeStruct((t, PEER_ROWS), jnp.int32),
                   jax.ShapeDtypeStruct((t, PEER_ROWS), f32)],
        name="peer_topk",
    )(q, keys)


def _peer_mid_kernel(part_ref, g_ref, o_ref):
    r = lax.broadcasted_iota(jnp.int32, (PEER_ROWS * SC_LANES, PEER_ROWS), 0) // SC_LANES
    c = lax.broadcasted_iota(jnp.int32, (PEER_ROWS * SC_LANES, PEER_ROWS), 1)
    fold = (r == c).astype(f32)
    dots = jnp.dot(part_ref[...], fold, precision=HIGHEST, preferred_element_type=f32)
    wgt = g_ref[...] * jax.nn.gelu(dots)
    r2 = lax.broadcasted_iota(jnp.int32, (PEER_ROWS, PEER_ROWS * SC_LANES), 0)
    c2 = lax.broadcasted_iota(jnp.int32, (PEER_ROWS, PEER_ROWS * SC_LANES), 1) // SC_LANES
    spread = (r2 == c2).astype(f32)
    o_ref[...] = jnp.dot(wgt, spread, precision=HIGHEST, preferred_element_type=f32)


def peer_mid(part, g, *, tm):
    t = part.shape[0]
    wide = pl.BlockSpec((tm, PEER_ROWS * SC_LANES), lambda i: (i, 0))
    return pl.pallas_call(
        _peer_mid_kernel,
        grid=(t // tm,),
        in_specs=[wide, pl.BlockSpec((tm, PEER_ROWS), lambda i: (i, 0))],
        out_specs=wide,
        out_shape=jax.ShapeDtypeStruct((t, PEER_ROWS * SC_LANES), f32),
        name="peer_mid",
    )(part, g)


def _unpack(w):
    lo = plsc.bitcast(w << 16, f32)
    hi = plsc.bitcast(w & jnp.int32(-65536), f32)
    return lo, hi


def _sc_worker_base(tokens_per_worker):
    wid = lax.axis_index("s") * SC_CORES + lax.axis_index("c")
    return wid * tokens_per_worker


def _sc_kernel(body, t, out_width, scratch):
    mesh = plsc.VectorSubcoreMesh(core_axis_name="c", subcore_axis_name="s")
    return pl.kernel(
        body, mesh=mesh,
        out_type=jax.ShapeDtypeStruct((t, out_width), f32),
        scratch_types=scratch,
        compiler_params=pltpu.CompilerParams(needs_layout_passes=False),
    )


def _sc_gather_loop(tab_hbm, idx_v, rows_v, sems, compute):
    def gather(tok, half):
        return pltpu.make_async_copy(tab_hbm.at[idx_v.at[tok, half]], rows_v.at[half], sems.at[half])

    gather(0, 0).start()

    def token(tok, carry):
        gather(tok, 1).start()
        gather(tok, 0).wait()
        compute(tok, 0, rows_v.at[0])

        @pl.when(tok + 1 < SC_TOK_BLOCK)
        def _():
            gather(tok + 1, 0).start()

        gather(tok, 1).wait()
        compute(tok, 1, rows_v.at[1])
        return carry

    lax.fori_loop(0, SC_TOK_BLOCK, token, 0)


def sc_peer_dots(x, idx, tab):
    t = x.shape[0]
    tpw = t // SC_WORKERS
    rpg = 8

    def body(x_hbm, idx_hbm, tab_hbm, out_hbm, x_v, idx_v, rows_v, out_v, sems):
        base = _sc_worker_base(tpw)

        def block(blk, carry):
            t0 = base + blk * SC_TOK_BLOCK
            pltpu.sync_copy(x_hbm.at[pl.ds(t0, SC_TOK_BLOCK)], x_v)
            pltpu.sync_copy(idx_hbm.at[pl.ds(t0, SC_TOK_BLOCK)], idx_v)

            def compute(tok, half, rows):
                def group(g, c2):
                    def words(j, accs):
                        xlo = x_v[tok, pl.ds(j * SC_LANES, SC_LANES)]
                        xhi = x_v[tok, pl.ds(HALF_W + j * SC_LANES, SC_LANES)]
                        new = []
                        for r in range(rpg):
                            lo, hi = _unpack(rows[g * rpg + r, pl.ds(j * SC_LANES, SC_LANES)])
                            new.append(accs[r] + lo * xlo + hi * xhi)
                        return tuple(new)

                    accs = lax.fori_loop(0, HALF_W // SC_LANES, words,
                                         tuple(jnp.zeros((SC_LANES,), f32) for _ in range(rpg)))
                    for r in range(rpg):
                        out_v[tok, pl.ds((half * SC_CHUNK_ROWS + g * rpg + r) * SC_LANES, SC_LANES)] = accs[r]
                    return c2

                lax.fori_loop(0, SC_CHUNK_ROWS // rpg, group, 0)

            _sc_gather_loop(tab_hbm, idx_v, rows_v, sems, compute)
            pltpu.sync_copy(out_v, out_hbm.at[pl.ds(t0, SC_TOK_BLOCK)])
            return carry

        lax.fori_loop(0, tpw // SC_TOK_BLOCK, block, 0)

    scratch = [pltpu.VMEM((SC_TOK_BLOCK, D_MODEL), f32),
               pltpu.VMEM((SC_TOK_BLOCK, 2, SC_CHUNK_ROWS), jnp.int32),
               pltpu.VMEM((2, SC_CHUNK_ROWS, HALF_W), jnp.int32),
               pltpu.VMEM((SC_TOK_BLOCK, PEER_ROWS * SC_LANES), f32),
               pltpu.SemaphoreType.DMA((2,))]
    return _sc_kernel(body, t, PEER_ROWS * SC_LANES, scratch)(x, idx, tab)


def sc_peer_mix(wgt, idx, tab):
    t = wgt.shape[0]
    tpw = t // SC_WORKERS
    wpg = 8
    ngroups = HALF_W // SC_LANES // wpg

    def body(w_hbm, idx_hbm, tab_hbm, out_hbm, w_v, idx_v, rows_v, out_v, sems):
        base = _sc_worker_base(tpw)

        def block(blk, carry):
            t0 = base + blk * SC_TOK_BLOCK
            pltpu.sync_copy(w_hbm.at[pl.ds(t0, SC_TOK_BLOCK)], w_v)
            pltpu.sync_copy(idx_hbm.at[pl.ds(t0, SC_TOK_BLOCK)], idx_v)

            def compute(tok, half, rows):
                def group(g, c2):
                    def out_at(jj, hw):
                        return out_v.at[tok, pl.ds(hw * HALF_W + (g * wpg + jj) * SC_LANES, SC_LANES)]

                    def row(k, accs):
                        wk = w_v[tok, pl.ds((half * SC_CHUNK_ROWS + k) * SC_LANES, SC_LANES)]
                        new = []
                        for jj in range(wpg):
                            lo, hi = _unpack(rows[k, pl.ds((g * wpg + jj) * SC_LANES, SC_LANES)])
                            new.append(accs[2 * jj] + wk * lo)
                            new.append(accs[2 * jj + 1] + wk * hi)
                        return tuple(new)

                    if half == 0:
                        init = tuple(jnp.zeros((SC_LANES,), f32) for _ in range(2 * wpg))
                    else:
                        init = tuple(out_at(jj, hw)[...] for jj in range(wpg) for hw in range(2))
                    accs = lax.fori_loop(0, SC_CHUNK_ROWS, row, init)
                    for jj in range(wpg):
                        for hw in range(2):
                            out_at(jj, hw)[...] = accs[2 * jj + hw]
                    return c2

                lax.fori_loop(0, ngroups, group, 0)

            _sc_gather_loop(tab_hbm, idx_v, rows_v, sems, compute)
            pltpu.sync_copy(out_v, out_hbm.at[pl.ds(t0, SC_TOK_BLOCK)])
            return carry

        lax.fori_loop(0, tpw // SC_TOK_BLOCK, block, 0)

    scratch = [pltpu.VMEM((SC_TOK_BLOCK, PEER_ROWS * SC_LANES), f32),
               pltpu.VMEM((SC_TOK_BLOCK, 2, SC_CHUNK_ROWS), jnp.int32),
               pltpu.VMEM((2, SC_CHUNK_ROWS, HALF_W), jnp.int32),
               pltpu.VMEM((SC_TOK_BLOCK, D_MODEL), f32),
               pltpu.SemaphoreType.DMA((2,))]
    return _sc_kernel(body, t, D_MODEL, scratch)(wgt, idx, tab)


def pack_table(tab):
    b = lax.bitcast_convert_type(tab.astype(bf16), jnp.uint16).astype(jnp.uint32)
    return lax.bitcast_convert_type(b[:, :HALF_W] | (b[:, HALF_W:] << 16), jnp.int32)


def peer_mix(hm, q, keys, u_packed, v_packed, *, tm):
    t = hm.shape[0]
    idx, g = peer_topk(q, keys, tm=tm)
    idx = idx.reshape(t, 2, SC_CHUNK_ROWS)
    part = sc_peer_dots(hm, idx, u_packed)
    wgt = peer_mid(part, g, tm=tm)
    return sc_peer_mix(wgt, idx, v_packed)


def _block_diag(w):
    nb, n, _ = w.shape
    eye = jnp.eye(nb, dtype=w.dtype)
    return (eye[:, None, :, None] * w[:, :, None, :]).reshape(nb * n, nb * n)


def _prep_weights(p):
    w = {}
    w["gla_main"] = p["gla_w_in"][:, :, :2 * GLA_DK + 2 * GLA_DV].astype(bf16)
    w["gla_lr"] = jnp.pad(p["gla_w_in"][:, :, 2 * GLA_DK + 2 * GLA_DV:],
                          ((0, 0), (0, 0), (0, LANES - GLA_RANK))).astype(bf16)
    w["gla_gate2"] = jnp.pad(p["gla_w_gate2"], ((0, 0), (0, LANES - GLA_RANK), (0, 0))).astype(bf16)
    w["gla_out"] = p["gla_w_out"].astype(bf16)
    w["rg_in"] = p["rg_w_in"].astype(bf16)
    w["rg_wa"] = jax.vmap(_block_diag)(p["rg_w_a"]).astype(bf16)
    w["rg_wx"] = jax.vmap(_block_diag)(p["rg_w_x"]).astype(bf16)
    w["rg_out"] = p["rg_w_out"].astype(bf16)
    w["peer_q"] = p["peer_w_q"].astype(bf16)
    w["peer_keys"] = p["peer_sub_keys"].astype(bf16)
    w["peer_u"] = jax.vmap(pack_table)(p["peer_u"])
    w["peer_v"] = jax.vmap(pack_table)(p["peer_v"])
    return w


def _trunk(x, mod, start, gla_s0, rg_h0, rg_conv0, p, w, *, decode):
    b, seq, _ = x.shape
    t = b * seq
    tm = min(512, t if decode else seq)
    xt = x.reshape(t, D_MODEL)
    if decode:
        chunk = 16
        bb = 4
    else:
        chunk = min(GLA_CHUNK, seq)
        bb = 1
    delta = gate = None
    gla_new, rg_h_new, rg_conv_new = [], [], []
    for i in range(DEPTH):
        m = mod[i].reshape(b, 6, D_MODEL)
        if decode:
            mods = [jnp.repeat(m[:, j], seq, axis=0) for j in range(6)]
        else:
            mods = [m[:, j:j + 1] for j in range(6)]
        sh1, sc1, g1, sh2, sc2, g2 = mods
        kw = dict(seq_len=seq, tm=tm)
        j = i // 2
        if i % 2 == 0:
            if delta is None:
                (y,) = norm_proj(xt, p["norm_mix_g"][i], sc1, sh1, w["gla_main"][j], **kw)
            else:
                xt, y = norm_proj(xt, p["norm_mix_g"][i], sc1, sh1, w["gla_main"][j], delta=delta, gate=gate, **kw)
            (lr,) = norm_proj(xt, p["norm_mix_g"][i], sc1, sh1, w["gla_lr"][j], **kw)
            y3, lr3 = y.reshape(b, seq, -1), lr.reshape(b, seq, LANES)
            if decode:
                padding = ((0, 0), (0, chunk - seq), (0, 0))
                y3, lr3 = jnp.pad(y3, padding), jnp.pad(lr3, padding)
            o, s_new = gla_mix(y3, lr3, w["gla_gate2"][j], p["gla_b_gate"][j], p["gla_onorm_g"][j],
                               gla_s0[j], bb=bb, chunk=chunk, valid=min(seq, chunk))
            gla_new.append(s_new)
            o = o[:, :seq].reshape(t, GLA_DV)
            xt = out_proj(o, w["gla_out"][j], xt, g1, **kw)
        else:
            xt, y = norm_proj(xt, p["norm_mix_g"][i], sc1, sh1, w["rg_in"][j], delta=delta, gate=gate, **kw)
            rg_args = (p["rg_conv_w"][j], p["rg_conv_b"][j], w["rg_wa"][j], p["rg_b_a"][j], w["rg_wx"][j],
                       p["rg_b_x"][j], p["rg_lambda"][j])
            if decode:
                y3 = y.reshape(b, seq, 2 * D_RNN).transpose(1, 0, 2)
                o, h_new = rglru_decode(y3, rg_h0[j], rg_conv0[j].transpose(1, 0, 2), *rg_args, start=start)
                xbr = y.reshape(b, seq, 2 * D_RNN)[:, :, D_RNN:]
                conv_new = jnp.concatenate([rg_conv0[j], xbr], axis=1)[:, seq:]
                o = o.transpose(1, 0, 2).reshape(t, D_RNN)
            else:
                o, h_new, conv_new = rglru_prefill(y.reshape(b, seq, 2 * D_RNN), rg_h0[j], rg_conv0[j], *rg_args,
                                                   start=start, tl=min(256, seq))
                o, h_new = o.reshape(t, D_RNN), h_new.reshape(b, D_RNN)
            rg_h_new.append(h_new)
            rg_conv_new.append(conv_new)
            xt = out_proj(o, w["rg_out"][j], xt, g1, **kw)
        hm, q = norm_proj(xt, p["norm_ffn_g"][i], sc2, sh2, w["peer_q"][i], want_h=True, **kw)
        delta = peer_mix(hm, q, w["peer_keys"][i], w["peer_u"][i], w["peer_v"][i], tm=min(256, t))
        gate = g2
    zeros = jnp.zeros_like(gate)
    xt, y = norm_proj(xt, p["final_norm_g"], zeros, zeros, delta=delta, gate=gate, want_h=True, seq_len=seq, tm=tm)
    return y.reshape(b, seq, D_MODEL), jnp.stack(gla_new), jnp.stack(rg_h_new), jnp.stack(rg_conv_new)


def kernel(x_prompt, x_sample, c_prompt, c_sample, state_gla, state_rglru_h, state_rglru_conv, ada_w, ada_b, norm_mix_g, norm_ffn_g, gla_w_in, gla_w_gate2, gla_b_gate, gla_onorm_g, gla_w_out, rg_w_in, rg_conv_w, rg_conv_b, rg_w_a, rg_b_a, rg_w_x, rg_b_x, rg_lambda, rg_w_out, peer_w_q, peer_sub_keys, peer_u, peer_v, final_norm_g):
    p = dict(norm_mix_g=norm_mix_g, norm_ffn_g=norm_ffn_g, gla_w_in=gla_w_in, gla_w_gate2=gla_w_gate2,
             gla_b_gate=gla_b_gate, gla_onorm_g=gla_onorm_g, gla_w_out=gla_w_out, rg_w_in=rg_w_in,
             rg_conv_w=rg_conv_w, rg_conv_b=rg_conv_b, rg_w_a=rg_w_a, rg_b_a=rg_b_a, rg_w_x=rg_w_x,
             rg_b_x=rg_b_x, rg_lambda=rg_lambda, rg_w_out=rg_w_out, peer_w_q=peer_w_q,
             peer_sub_keys=peer_sub_keys, peer_u=peer_u, peer_v=peer_v, final_norm_g=final_norm_g)
    w = _prep_weights(p)
    nb_p, nb_s = x_prompt.shape[0], x_sample.shape[0]
    n_gla, n_rg = state_gla.shape[0], state_rglru_h.shape[0]
    mod = adaln(jnp.concatenate([c_prompt, c_sample], axis=0), ada_w, ada_b)
    gla0 = jnp.zeros((n_gla, nb_p) + state_gla.shape[2:], f32)
    h0 = jnp.zeros((n_rg, nb_p, D_RNN), f32)
    conv0 = jnp.zeros((n_rg, nb_p, CONV_W - 1, D_RNN), f32)
    y_p, gla_p, h_p, conv_p = _trunk(x_prompt, mod[:, :nb_p], 0, gla0, h0, conv0, p, w, decode=False)
    y_s, gla_s, h_s, conv_s = _trunk(x_sample, mod[:, nb_p:], PAST_LEN, state_gla, state_rglru_h, state_rglru_conv,
                                     p, w, decode=True)
    return (y_p, y_s, gla_p, gla_s, h_p, h_s, conv_p, conv_s)
```

```python
import functools

import jax
import jax.numpy as jnp
from jax import lax
from jax.experimental import pallas as pl
from jax.experimental.pallas import tpu as pltpu
from jax.experimental.pallas import tpu_sc as plsc

f32 = jnp.float32
bf16 = jnp.bfloat16
HIGHEST = lax.Precision.HIGHEST

D_MODEL = 1024
DEPTH = 4
GLA_HEADS = 4
GLA_DK = 512
GLA_DV = 1024
GLA_HDK = 128
GLA_HDV = 256
GLA_RANK = 16
GLA_TAU = 16.0
GLA_CHUNK = 64
D_RNN = 1280
RG_BLOCKS = 8
RG_BLOCK = 160
CONV_W = 4
RG_C = 8.0
PEER_HEADS = 8
PEER_NKEYS = 128
PEER_DHALF = 128
PEER_TOPK = 16
PEER_ROWS = PEER_HEADS * PEER_TOPK
PAST_LEN = 16384
EPS = 1e-6

SC_CORES = 2
SC_SUBCORES = 16
SC_LANES = 16
SC_WORKERS = SC_CORES * SC_SUBCORES
LANES = 128
HALF_W = D_MODEL // 2
SC_CHUNK_ROWS = 64
SC_TOK_BLOCK = 8
PROMPT_STREAMS = 2


def _rms(x, gain):
    return x * lax.rsqrt(jnp.mean(x * x, axis=-1, keepdims=True) + EPS) * gain


def _adaln_kernel(c_ref, w_ref, b_ref, o_ref):
    cond = jax.nn.silu(c_ref[...])
    o_ref[0] = jnp.dot(cond, w_ref[0], precision=HIGHEST, preferred_element_type=f32) + b_ref[0]


def adaln(c, ada_w, ada_b):
    bc = c.shape[0]
    tn = 1024
    return pl.pallas_call(
        _adaln_kernel,
        grid=(DEPTH, 6 * D_MODEL // tn),
        in_specs=[
            pl.BlockSpec((bc, D_MODEL), lambda i, j: (0, 0)),
            pl.BlockSpec((1, D_MODEL, tn), lambda i, j: (i, 0, j)),
            pl.BlockSpec((1, 1, tn), lambda i, j: (i, 0, j)),
        ],
        out_specs=pl.BlockSpec((1, bc, tn), lambda i, j: (i, 0, j)),
        out_shape=jax.ShapeDtypeStruct((DEPTH, bc, 6 * D_MODEL), f32),
        name="adaln",
    )(c, ada_w, ada_b.reshape(DEPTH, 1, 6 * D_MODEL))


def _proj_kernel(*refs, has_delta, has_w, want_h):
    it = iter(refs)
    x_ref = next(it)
    delta_ref = next(it) if has_delta else None
    gate_ref = next(it) if has_delta else None
    g_ref, sc_ref, sh_ref = next(it), next(it), next(it)
    w_ref = next(it) if has_w else None
    xo_ref = next(it) if has_delta else None
    h_ref = next(it) if want_h else None
    y_ref = next(it) if has_w else None
    hb_ref = next(it) if has_w else None

    def prologue():
        x = x_ref[...]
        if has_delta:
            x = x + gate_ref[...].reshape(-1, D_MODEL) * delta_ref[...]
            xo_ref[...] = x
        h = _rms(x, g_ref[...])
        h = h * (1.0 + sc_ref[...].reshape(-1, D_MODEL)) + sh_ref[...].reshape(-1, D_MODEL)
        if want_h:
            h_ref[...] = h
        if has_w:
            hb_ref[...] = h.astype(bf16)

    if has_w:
        pl.when(pl.program_id(1) == 0)(prologue)
        y_ref[...] = jnp.dot(hb_ref[...], w_ref[...], preferred_element_type=f32)
    else:
        prologue()


def _mod_spec(mod, tm, seq_len):
    if mod.ndim == 3:
        per = seq_len // tm
        return pl.BlockSpec((1, 1, D_MODEL), lambda i, j: (i // per, 0, 0))
    return pl.BlockSpec((tm, D_MODEL), lambda i, j: (i, 0))


def norm_proj(x, gain, scale, shift, w=None, *, delta=None, gate=None, want_h=False, seq_len, tm, tn=512):
    t = x.shape[0]
    has_delta, has_w = delta is not None, w is not None
    n = w.shape[1] if has_w else 0
    tn = min(tn, n) if has_w else 0
    grid = (t // tm, n // tn if has_w else 1)
    row = pl.BlockSpec((tm, D_MODEL), lambda i, j: (i, 0))
    args, specs = [x], [row]
    if has_delta:
        args += [delta, gate]
        specs += [row, _mod_spec(gate, tm, seq_len)]
    args += [gain.reshape(1, D_MODEL), scale, shift]
    specs += [pl.BlockSpec((1, D_MODEL), lambda i, j: (0, 0)), _mod_spec(scale, tm, seq_len),
              _mod_spec(shift, tm, seq_len)]
    out_shapes, out_specs, scratch = [], [], []
    if has_delta:
        out_shapes.append(jax.ShapeDtypeStruct((t, D_MODEL), f32))
        out_specs.append(row)
    if want_h:
        out_shapes.append(jax.ShapeDtypeStruct((t, D_MODEL), f32))
        out_specs.append(row)
    if has_w:
        args.append(w)
        specs.append(pl.BlockSpec((D_MODEL, tn), lambda i, j: (0, j)))
        out_shapes.append(jax.ShapeDtypeStruct((t, n), f32))
        out_specs.append(pl.BlockSpec((tm, tn), lambda i, j: (i, j)))
        scratch.append(pltpu.VMEM((tm, D_MODEL), bf16))
    return pl.pallas_call(
        functools.partial(_proj_kernel, has_delta=has_delta, has_w=has_w, want_h=want_h),
        grid=grid, in_specs=specs, out_specs=out_specs, out_shape=out_shapes,
        scratch_shapes=scratch,
        compiler_params=pltpu.CompilerParams(dimension_semantics=("parallel", "arbitrary")),
        name="norm_proj",
    )(*args)


def _out_proj_kernel(a_ref, w_ref, x_ref, gate_ref, o_ref):
    y = jnp.dot(a_ref[...], w_ref[...], preferred_element_type=f32)
    o_ref[...] = x_ref[...] + gate_ref[...].reshape(-1, D_MODEL) * y


def out_proj(a, w, x, gate, *, seq_len, tm):
    t, k = a.shape
    row = pl.BlockSpec((tm, D_MODEL), lambda i, j: (i, 0))
    return pl.pallas_call(
        _out_proj_kernel,
        grid=(t // tm, 1),
        in_specs=[pl.BlockSpec((tm, k), lambda i, j: (i, 0)),
                  pl.BlockSpec((k, D_MODEL), lambda i, j: (0, 0)),
                  row, _mod_spec(gate, tm, seq_len)],
        out_specs=row,
        out_shape=jax.ShapeDtypeStruct((t, D_MODEL), f32),
        name="out_proj",
    )(a, w, x, gate)


def _gla_kernel(q_ref, k_ref, v_ref, g_ref, lr_ref, wg_ref, bg_ref, on_ref, s0_ref,
                o_ref, so_ref, s_scr, *, bb, chunk, valid):
    c = pl.program_id(1)

    @pl.when(c == 0)
    def _():
        s_scr[...] = s0_ref[...]

    rows = lax.broadcasted_iota(jnp.int32, (chunk, chunk), 0)
    cols = lax.broadcasted_iota(jnp.int32, (chunk, chunk), 1)
    causal = rows >= cols
    tri = causal.astype(f32)
    ones = jnp.ones((chunk, GLA_HDV), f32)
    nt = (((1,), (1,)), ((), ()))
    tn = (((0,), (0,)), ((), ()))
    for bi in range(bb):
        gate_in = jnp.dot(lr_ref[bi].astype(bf16), wg_ref[...], preferred_element_type=f32) + bg_ref[...]
        log_a = jax.nn.log_sigmoid(gate_in) / GLA_TAU
        if valid < chunk:
            log_a = jnp.where(lax.broadcasted_iota(jnp.int32, log_a.shape, 0) < valid, log_a, 0.0)
        for h in range(GLA_HEADS):
            ks = slice(h * GLA_HDK, (h + 1) * GLA_HDK)
            vs = slice(h * GLA_HDV, (h + 1) * GLA_HDV)
            la = log_a[:, ks]
            b = jnp.dot(tri, la, precision=HIGHEST, preferred_element_type=f32)
            q = q_ref[bi, :, ks] * (GLA_HDK ** -0.5)
            k = k_ref[bi, :, ks]
            v = v_ref[bi, :, vs].astype(bf16)
            qd = (q * jnp.exp(b)).astype(bf16)
            kd = (k * jnp.exp(-b)).astype(bf16)
            scores = lax.dot_general(qd, kd, nt, preferred_element_type=f32)
            scores = jnp.where(causal, scores, 0.0).astype(bf16)
            s_old = s_scr[bi, h]
            o = (jnp.dot(scores, v, preferred_element_type=f32)
                 + jnp.dot(qd, s_old.astype(bf16), preferred_element_type=f32))
            b_end = b[chunk - 1:chunk, :]
            k_end = (k * jnp.exp(b_end - b)).astype(bf16)
            decay = jnp.exp(lax.dot_general(la, ones, tn, precision=HIGHEST, preferred_element_type=f32))
            s_scr[bi, h] = s_old * decay + lax.dot_general(k_end, v, tn, preferred_element_type=f32)
            on = _rms(o, on_ref[...])
            o_ref[bi, :, vs] = (on * jax.nn.silu(g_ref[bi, :, vs])).astype(o_ref.dtype)

    @pl.when(c == pl.num_programs(1) - 1)
    def _():
        so_ref[...] = s_scr[...]


def gla_mix(y, lr, w_gate2p, b_gate, onorm_g, s0, *, bb, chunk, valid):
    b, seq, _ = y.shape
    grid = (b // bb, seq // chunk)

    def col(width, blk):
        return pl.BlockSpec((bb, chunk, width), lambda i, c: (i, c, blk))

    st = pl.BlockSpec((bb, GLA_HEADS, GLA_HDK, GLA_HDV), lambda i, c: (i, 0, 0, 0))
    return pl.pallas_call(
        functools.partial(_gla_kernel, bb=bb, chunk=chunk, valid=valid),
        grid=grid,
        in_specs=[col(GLA_DK, 0), col(GLA_DK, 1), col(GLA_DV, 1), col(GLA_DV, 2),
                  pl.BlockSpec((bb, chunk, LANES), lambda i, c: (i, c, 0)),
                  pl.BlockSpec((LANES, GLA_DK), lambda i, c: (0, 0)),
                  pl.BlockSpec((1, GLA_DK), lambda i, c: (0, 0)),
                  pl.BlockSpec((1, GLA_HDV), lambda i, c: (0, 0)),
                  st],
        out_specs=[pl.BlockSpec((bb, chunk, GLA_DV), lambda i, c: (i, c, 0)), st],
        out_shape=[jax.ShapeDtypeStruct((b, seq, GLA_DV), bf16),
                   jax.ShapeDtypeStruct(s0.shape, f32)],
        scratch_shapes=[pltpu.VMEM((bb, GLA_HEADS, GLA_HDK, GLA_HDV), f32)],
        compiler_params=pltpu.CompilerParams(dimension_semantics=("parallel", "arbitrary")),
        name="gla_mix",
    )(y, y, y, y, lr, w_gate2p, b_gate.reshape(1, GLA_DK), onorm_g.reshape(1, GLA_HDV), s0)


def _rg_gates(xconv, wa_ref, ba_ref, wx_ref, bx_ref, lam_ref):
    xb = xconv.astype(bf16)
    gate_r = jax.nn.sigmoid(jnp.dot(xb, wa_ref[...], preferred_element_type=f32) + ba_ref[...])
    gate_i = jax.nn.sigmoid(jnp.dot(xb, wx_ref[...], preferred_element_type=f32) + bx_ref[...])
    log_a = -RG_C * gate_r * jax.nn.softplus(-lam_ref[...])
    a = jnp.exp(log_a)
    mult = jnp.sqrt(-jnp.tanh(log_a) * (a * a + 1.0))
    return a, mult, gate_i


def _rg_prefill_kernel(y_ref, xb_ref, h0_ref, c0_ref, cw_ref, cb_ref, wa_ref, ba_ref, wx_ref, bx_ref,
                       lam_ref, o_ref, ho_ref, co_ref, buf, h_scr, *, tl, start):
    t = pl.program_id(1)
    top = 8

    @pl.when(t == 0)
    def _():
        h_scr[...] = h0_ref[0]
        buf[top - (CONV_W - 1):top, :] = c0_ref[0]

    x = xb_ref[0]
    buf[top:top + tl, :] = x
    xconv = cb_ref[...] + sum(buf[top - (CONV_W - 1) + j:top - (CONV_W - 1) + j + tl, :] * cw_ref[j:j + 1, :]
                              for j in range(CONV_W))
    a, mult, gate_i = _rg_gates(xconv, wa_ref, ba_ref, wx_ref, bx_ref, lam_ref)
    row = lax.broadcasted_iota(jnp.int32, (tl, D_RNN), 0)
    reset = (row + t * tl + start) == 0
    a = jnp.where(reset, 0.0, a)
    mult = jnp.where(reset, 1.0, mult)
    b = mult * gate_i * xconv
    s = 1
    while s < tl:
        keep = row >= s
        b = jnp.where(keep, a * pltpu.roll(b, s, 0) + b, b)
        a = jnp.where(keep, a * pltpu.roll(a, s, 0), a)
        s *= 2
    hs = a * h_scr[...] + b
    o_ref[0] = (hs * jax.nn.gelu(y_ref[0])).astype(o_ref.dtype)
    h_scr[...] = hs[tl - 1:tl, :]
    buf[top - (CONV_W - 1):top, :] = x[tl - (CONV_W - 1):tl, :]

    @pl.when(t == pl.num_programs(1) - 1)
    def _():
        ho_ref[0] = hs[tl - 1:tl, :]
        co_ref[0] = x[tl - (CONV_W - 1):tl, :]


def _rg_vec(v):
    return v.reshape(1, D_RNN)


def rglru_prefill(y, h0, conv0, conv_w, conv_b, wa, ba, wx, bx, lam, *, start, tl):
    b, seq, _ = y.shape
    assert seq % tl == 0 and tl >= CONV_W - 1
    vec = pl.BlockSpec((1, D_RNN), lambda i, t: (0, 0))
    mat = pl.BlockSpec((D_RNN, D_RNN), lambda i, t: (0, 0))
    return pl.pallas_call(
        functools.partial(_rg_prefill_kernel, tl=tl, start=start),
        grid=(b, seq // tl),
        in_specs=[pl.BlockSpec((1, tl, D_RNN), lambda i, t: (i, t, 0)),
                  pl.BlockSpec((1, tl, D_RNN), lambda i, t: (i, t, 1)),
                  pl.BlockSpec((1, 1, D_RNN), lambda i, t: (i, 0, 0)),
                  pl.BlockSpec((1, CONV_W - 1, D_RNN), lambda i, t: (i, 0, 0)),
                  pl.BlockSpec((CONV_W, D_RNN), lambda i, t: (0, 0)),
                  vec, mat, vec, mat, vec, vec],
        out_specs=[pl.BlockSpec((1, tl, D_RNN), lambda i, t: (i, t, 0)),
                   pl.BlockSpec((1, 1, D_RNN), lambda i, t: (i, 0, 0)),
                   pl.BlockSpec((1, CONV_W - 1, D_RNN), lambda i, t: (i, 0, 0))],
        out_shape=[jax.ShapeDtypeStruct((b, seq, D_RNN), bf16),
                   jax.ShapeDtypeStruct((b, 1, D_RNN), f32),
                   jax.ShapeDtypeStruct((b, CONV_W - 1, D_RNN), f32)],
        scratch_shapes=[pltpu.VMEM((8 + tl, D_RNN), f32), pltpu.VMEM((1, D_RNN), f32)],
        compiler_params=pltpu.CompilerParams(dimension_semantics=("parallel", "arbitrary")),
        name="rglru_prefill",
    )(y, y, h0.reshape(b, 1, D_RNN), conv0, conv_w, _rg_vec(conv_b), wa, _rg_vec(ba), wx, _rg_vec(bx),
      _rg_vec(lam))


def _rg_decode_kernel(y_ref, xb_ref, h0_ref, c0_ref, cw_ref, cb_ref, wa_ref, ba_ref, wx_ref, bx_ref,
                      lam_ref, o_ref, ho_ref, *, steps, start):
    xc = [c0_ref[j] for j in range(CONV_W - 1)] + [xb_ref[s] for s in range(steps)]
    h = h0_ref[...]
    for s in range(steps):
        xconv = cb_ref[...] + sum(xc[s + j] * cw_ref[j:j + 1, :] for j in range(CONV_W))
        a, mult, gate_i = _rg_gates(xconv, wa_ref, ba_ref, wx_ref, bx_ref, lam_ref)
        if start + s == 0:
            a, mult = 0.0, 1.0
        h = a * h + mult * gate_i * xconv
        o_ref[s] = (h * jax.nn.gelu(y_ref[s])).astype(o_ref.dtype)
    ho_ref[...] = h


def rglru_decode(y, h0, conv0, conv_w, conv_b, wa, ba, wx, bx, lam, *, start):
    steps, b, _ = y.shape
    whole = lambda *shape: pl.BlockSpec(shape, lambda i: (0,) * len(shape))
    vec, mat = whole(1, D_RNN), whole(D_RNN, D_RNN)
    return pl.pallas_call(
        functools.partial(_rg_decode_kernel, steps=steps, start=start),
        grid=(1,),
        in_specs=[pl.BlockSpec((steps, b, D_RNN), lambda i: (0, 0, 0)),
                  pl.BlockSpec((steps, b, D_RNN), lambda i: (0, 0, 1)),
                  whole(b, D_RNN), whole(CONV_W - 1, b, D_RNN), whole(CONV_W, D_RNN),
                  vec, mat, vec, mat, vec, vec],
        out_specs=[whole(steps, b, D_RNN), whole(b, D_RNN)],
        out_shape=[jax.ShapeDtypeStruct((steps, b, D_RNN), bf16),
                   jax.ShapeDtypeStruct((b, D_RNN), f32)],
        name="rglru_decode",
    )(y, y, h0, conv0, conv_w, _rg_vec(conv_b), wa, _rg_vec(ba), wx, _rg_vec(bx), _rg_vec(lam))


def _top_rows(v, nrows, ntok):
    rowi = lax.broadcasted_iota(jnp.int32, (nrows, ntok), 0)
    vals, idxs = [], []
    for _ in range(PEER_TOPK):
        m = jnp.max(v, axis=0, keepdims=True)
        i = jnp.min(jnp.where(v == m, rowi, nrows), axis=0, keepdims=True)
        vals.append(m)
        idxs.append(i)
        v = jnp.where(rowi == i, -jnp.inf, v)
    return jnp.concatenate(vals, axis=0), jnp.concatenate(idxs, axis=0)


def _peer_topk_kernel(q_ref, keys_ref, idx_ref, g_ref, *, tm):
    nt = (((1,), (1,)), ((), ()))
    idx_rows, g_rows = [], []
    for h in range(PEER_HEADS):
        top = []
        for p in range(2):
            col = (h * 2 + p) * PEER_DHALF
            qs = q_ref[:, col:col + PEER_DHALF].astype(bf16)
            s = lax.dot_general(keys_ref[h, p], qs, nt, preferred_element_type=f32)
            top.append(_top_rows(s, PEER_NKEYS, tm))
        (s0, i0), (s1, i1) = top
        cand = jnp.concatenate([s0[a:a + 1] + s1 for a in range(PEER_TOPK)], axis=0)
        cidx = jnp.concatenate([i0[a:a + 1] * PEER_NKEYS + i1 for a in range(PEER_TOPK)], axis=0)
        best, pos = _top_rows(cand, PEER_TOPK * PEER_TOPK, tm)
        rowi = lax.broadcasted_iota(jnp.int32, (PEER_TOPK * PEER_TOPK, tm), 0)
        idx = jnp.concatenate(
            [jnp.max(jnp.where(rowi == pos[k:k + 1], cidx, -1), axis=0, keepdims=True)
             for k in range(PEER_TOPK)], axis=0)
        e = jnp.exp(best - best[0:1])
        g_rows.append(e / jnp.sum(e, axis=0, keepdims=True))
        idx_rows.append(idx.astype(f32))
    idx_ref[...] = jnp.concatenate(idx_rows, axis=0).T.astype(jnp.int32)
    g_ref[...] = jnp.concatenate(g_rows, axis=0).T


def peer_topk(q, keys, *, tm):
    t = q.shape[0]
    out = pl.BlockSpec((tm, PEER_ROWS), lambda i: (i, 0))
    return pl.pallas_call(
        functools.partial(_peer_topk_kernel, tm=tm),
        grid=(t // tm,),
        in_specs=[pl.BlockSpec((tm, 2 * PEER_HEADS * PEER_DHALF), lambda i: (i, 0)),
                  pl.BlockSpec((PEER_HEADS, 2, PEER_NKEYS, PEER_DHALF), lambda i: (0, 0, 0, 0))],
        out_specs=[out, out],
        out_shape=[jax.ShapeDtypeStruct((t, PEER_ROWS), jnp.int32),
                   jax.ShapeDtypeStruct((t, PEER_ROWS), f32)],
        name="peer_topk",
    )(q, keys)


def _peer_mid_kernel(part_ref, g_ref, o_ref):
    r = lax.broadcasted_iota(jnp.int32, (PEER_ROWS * SC_LANES, PEER_ROWS), 0) // SC_LANES
    c = lax.broadcasted_iota(jnp.int32, (PEER_ROWS * SC_LANES, PEER_ROWS), 1)
    fold = (r == c).astype(f32)
    dots = jnp.dot(part_ref[...], fold, precision=HIGHEST, preferred_element_type=f32)
    wgt = g_ref[...] * jax.nn.gelu(dots)
    r2 = lax.broadcasted_iota(jnp.int32, (PEER_ROWS, PEER_ROWS * SC_LANES), 0)
    c2 = lax.broadcasted_iota(jnp.int32, (PEER_ROWS, PEER_ROWS * SC_LANES), 1) // SC_LANES
    spread = (r2 == c2).astype(f32)
    o_ref[...] = jnp.dot(wgt, spread, precision=HIGHEST, preferred_element_type=f32)


def peer_mid(part, g, *, tm):
    t = part.shape[0]
    wide = pl.BlockSpec((tm, PEER_ROWS * SC_LANES), lambda i: (i, 0))
    return pl.pallas_call(
        _peer_mid_kernel,
        grid=(t // tm,),
        in_specs=[wide, pl.BlockSpec((tm, PEER_ROWS), lambda i: (i, 0))],
        out_specs=wide,
        out_shape=jax.ShapeDtypeStruct((t, PEER_ROWS * SC_LANES), f32),
        name="peer_mid",
    )(part, g)


def _unpack(w):
    lo = plsc.bitcast(w << 16, f32)
    hi = plsc.bitcast(w & jnp.int32(-65536), f32)
    return lo, hi


def _sc_worker_base(tokens_per_worker):
    wid = lax.axis_index("s") * SC_CORES + lax.axis_index("c")
    return wid * tokens_per_worker


def _sc_kernel(body, t, out_width, scratch):
    mesh = plsc.VectorSubcoreMesh(core_axis_name="c", subcore_axis_name="s")
    return pl.kernel(
        body, mesh=mesh,
        out_type=jax.ShapeDtypeStruct((t, out_width), f32),
        scratch_types=scratch,
        compiler_params=pltpu.CompilerParams(needs_layout_passes=False),
    )


def _sc_gather_loop(tab_hbm, idx_v, rows_v, sems, compute):
    def gather(tok, half):
        return pltpu.make_async_copy(tab_hbm.at[idx_v.at[tok, half]], rows_v.at[half], sems.at[half])

    gather(0, 0).start()

    def token(tok, carry):
        gather(tok, 1).start()
        gather(tok, 0).wait()
        compute(tok, 0, rows_v.at[0])

        @pl.when(tok + 1 < SC_TOK_BLOCK)
        def _():
            gather(tok + 1, 0).start()

        gather(tok, 1).wait()
        compute(tok, 1, rows_v.at[1])
        return carry

    lax.fori_loop(0, SC_TOK_BLOCK, token, 0)


def sc_peer_dots(x, idx, tab):
    t = x.shape[0]
    tpw = t // SC_WORKERS
    rpg = 8

    def body(x_hbm, idx_hbm, tab_hbm, out_hbm, x_v, idx_v, rows_v, out_v, sems):
        base = _sc_worker_base(tpw)

        def block(blk, carry):
            t0 = base + blk * SC_TOK_BLOCK
            pltpu.sync_copy(x_hbm.at[pl.ds(t0, SC_TOK_BLOCK)], x_v)
            pltpu.sync_copy(idx_hbm.at[pl.ds(t0, SC_TOK_BLOCK)], idx_v)

            def compute(tok, half, rows):
                def group(g, c2):
                    def words(j, accs):
                        xlo = x_v[tok, pl.ds(j * SC_LANES, SC_LANES)]
                        xhi = x_v[tok, pl.ds(HALF_W + j * SC_LANES, SC_LANES)]
                        new = []
                        for r in range(rpg):
                            lo, hi = _unpack(rows[g * rpg + r, pl.ds(j * SC_LANES, SC_LANES)])
                            new.append(accs[r] + lo * xlo + hi * xhi)
                        return tuple(new)

                    accs = lax.fori_loop(0, HALF_W // SC_LANES, words,
                                         tuple(jnp.zeros((SC_LANES,), f32) for _ in range(rpg)))
                    for r in range(rpg):
                        out_v[tok, pl.ds((half * SC_CHUNK_ROWS + g * rpg + r) * SC_LANES, SC_LANES)] = accs[r]
                    return c2

                lax.fori_loop(0, SC_CHUNK_ROWS // rpg, group, 0)

            _sc_gather_loop(tab_hbm, idx_v, rows_v, sems, compute)
            pltpu.sync_copy(out_v, out_hbm.at[pl.ds(t0, SC_TOK_BLOCK)])
            return carry

        lax.fori_loop(0, tpw // SC_TOK_BLOCK, block, 0)

    scratch = [pltpu.VMEM((SC_TOK_BLOCK, D_MODEL), f32),
               pltpu.VMEM((SC_TOK_BLOCK, 2, SC_CHUNK_ROWS), jnp.int32),
               pltpu.VMEM((2, SC_CHUNK_ROWS, HALF_W), jnp.int32),
               pltpu.VMEM((SC_TOK_BLOCK, PEER_ROWS * SC_LANES), f32),
               pltpu.SemaphoreType.DMA((2,))]
    return _sc_kernel(body, t, PEER_ROWS * SC_LANES, scratch)(x, idx, tab)


def sc_peer_mix(wgt, idx, tab):
    t = wgt.shape[0]
    tpw = t // SC_WORKERS
    wpg = 8
    ngroups = HALF_W // SC_LANES // wpg

    def body(w_hbm, idx_hbm, tab_hbm, out_hbm, w_v, idx_v, rows_v, out_v, sems):
        base = _sc_worker_base(tpw)

        def block(blk, carry):
            t0 = base + blk * SC_TOK_BLOCK
            pltpu.sync_copy(w_hbm.at[pl.ds(t0, SC_TOK_BLOCK)], w_v)
            pltpu.sync_copy(idx_hbm.at[pl.ds(t0, SC_TOK_BLOCK)], idx_v)

            def compute(tok, half, rows):
                def group(g, c2):
                    def out_at(jj, hw):
                        return out_v.at[tok, pl.ds(hw * HALF_W + (g * wpg + jj) * SC_LANES, SC_LANES)]

                    def row(k, accs):
                        wk = w_v[tok, pl.ds((half * SC_CHUNK_ROWS + k) * SC_LANES, SC_LANES)]
                        new = []
                        for jj in range(wpg):
                            lo, hi = _unpack(rows[k, pl.ds((g * wpg + jj) * SC_LANES, SC_LANES)])
                            new.append(accs[2 * jj] + wk * lo)
                            new.append(accs[2 * jj + 1] + wk * hi)
                        return tuple(new)

                    if half == 0:
                        init = tuple(jnp.zeros((SC_LANES,), f32) for _ in range(2 * wpg))
                    else:
                        init = tuple(out_at(jj, hw)[...] for jj in range(wpg) for hw in range(2))
                    accs = lax.fori_loop(0, SC_CHUNK_ROWS, row, init)
                    for jj in range(wpg):
                        for hw in range(2):
                            out_at(jj, hw)[...] = accs[2 * jj + hw]
                    return c2

                lax.fori_loop(0, ngroups, group, 0)

            _sc_gather_loop(tab_hbm, idx_v, rows_v, sems, compute)
            pltpu.sync_copy(out_v, out_hbm.at[pl.ds(t0, SC_TOK_BLOCK)])
            return carry

        lax.fori_loop(0, tpw // SC_TOK_BLOCK, block, 0)

    scratch = [pltpu.VMEM((SC_TOK_BLOCK, PEER_ROWS * SC_LANES), f32),
               pltpu.VMEM((SC_TOK_BLOCK, 2, SC_CHUNK_ROWS), jnp.int32),
               pltpu.VMEM((2, SC_CHUNK_ROWS, HALF_W), jnp.int32),
               pltpu.VMEM((SC_TOK_BLOCK, D_MODEL), f32),
               pltpu.SemaphoreType.DMA((2,))]
    return _sc_kernel(body, t, D_MODEL, scratch)(wgt, idx, tab)


def pack_table(tab):
    b = lax.bitcast_convert_type(tab.astype(bf16), jnp.uint16).astype(jnp.uint32)
    return lax.bitcast_convert_type(b[:, :HALF_W] | (b[:, HALF_W:] << 16), jnp.int32)


def peer_mix(hm, q, keys, u_packed, v_packed, *, tm):
    t = hm.shape[0]
    idx, g = peer_topk(q, keys, tm=tm)
    idx = idx.reshape(t, 2, SC_CHUNK_ROWS)
    part = sc_peer_dots(hm, idx, u_packed)
    wgt = peer_mid(part, g, tm=tm)
    return sc_peer_mix(wgt, idx, v_packed)


def _block_diag(w):
    nb, n, _ = w.shape
    eye = jnp.eye(nb, dtype=w.dtype)
    return (eye[:, None, :, None] * w[:, :, None, :]).reshape(nb * n, nb * n)


def _prep_weights(p):
    w = {}
    w["gla_main"] = p["gla_w_in"][:, :, :2 * GLA_DK + 2 * GLA_DV].astype(bf16)
    w["gla_lr"] = jnp.pad(p["gla_w_in"][:, :, 2 * GLA_DK + 2 * GLA_DV:],
                          ((0, 0), (0, 0), (0, LANES - GLA_RANK))).astype(bf16)
    w["gla_gate2"] = jnp.pad(p["gla_w_gate2"], ((0, 0), (0, LANES - GLA_RANK), (0, 0))).astype(bf16)
    w["gla_out"] = p["gla_w_out"].astype(bf16)
    w["rg_in"] = p["rg_w_in"].astype(bf16)
    w["rg_wa"] = jax.vmap(_block_diag)(p["rg_w_a"]).astype(bf16)
    w["rg_wx"] = jax.vmap(_block_diag)(p["rg_w_x"]).astype(bf16)
    w["rg_out"] = p["rg_w_out"].astype(bf16)
    w["peer_q"] = p["peer_w_q"].astype(bf16)
    w["peer_keys"] = p["peer_sub_keys"].astype(bf16)
    w["peer_u"] = jax.vmap(pack_table)(p["peer_u"])
    w["peer_v"] = jax.vmap(pack_table)(p["peer_v"])
    return w


def _trunk(x, mod, start, gla_s0, rg_h0, rg_conv0, p, w, *, decode):
    b, seq, _ = x.shape
    t = b * seq
    tm = min(512, t if decode else seq)
    xt = x.reshape(t, D_MODEL)
    if decode:
        chunk = 16
        bb = 4
    else:
        chunk = min(GLA_CHUNK, seq)
        bb = 1
    delta = gate = None
    gla_new, rg_h_new, rg_conv_new = [], [], []
    for i in range(DEPTH):
        m = mod[i].reshape(b, 6, D_MODEL)
        if decode:
            mods = [jnp.repeat(m[:, j], seq, axis=0) for j in range(6)]
        else:
            mods = [m[:, j:j + 1] for j in range(6)]
        sh1, sc1, g1, sh2, sc2, g2 = mods
        kw = dict(seq_len=seq, tm=tm)
        j = i // 2
        if i % 2 == 0:
            if delta is None:
                (y,) = norm_proj(xt, p["norm_mix_g"][i], sc1, sh1, w["gla_main"][j], **kw)
            else:
                xt, y = norm_proj(xt, p["norm_mix_g"][i], sc1, sh1, w["gla_main"][j], delta=delta, gate=gate, **kw)
            (lr,) = norm_proj(xt, p["norm_mix_g"][i], sc1, sh1, w["gla_lr"][j], **kw)
            y3, lr3 = y.reshape(b, seq, -1), lr.reshape(b, seq, LANES)
            if decode:
                padding = ((0, 0), (0, chunk - seq), (0, 0))
                y3, lr3 = jnp.pad(y3, padding), jnp.pad(lr3, padding)
            o, s_new = gla_mix(y3, lr3, w["gla_gate2"][j], p["gla_b_gate"][j], p["gla_onorm_g"][j],
                               gla_s0[j], bb=bb, chunk=chunk, valid=min(seq, chunk))
            gla_new.append(s_new)
            o = o[:, :seq].reshape(t, GLA_DV)
            xt = out_proj(o, w["gla_out"][j], xt, g1, **kw)
        else:
            xt, y = norm_proj(xt, p["norm_mix_g"][i], sc1, sh1, w["rg_in"][j], delta=delta, gate=gate, **kw)
            rg_args = (p["rg_conv_w"][j], p["rg_conv_b"][j], w["rg_wa"][j], p["rg_b_a"][j], w["rg_wx"][j],
                       p["rg_b_x"][j], p["rg_lambda"][j])
            if decode:
                y3 = y.reshape(b, seq, 2 * D_RNN).transpose(1, 0, 2)
                o, h_new = rglru_decode(y3, rg_h0[j], rg_conv0[j].transpose(1, 0, 2), *rg_args, start=start)
                xbr = y.reshape(b, seq, 2 * D_RNN)[:, :, D_RNN:]
                conv_new = jnp.concatenate([rg_conv0[j], xbr], axis=1)[:, seq:]
                o = o.transpose(1, 0, 2).reshape(t, D_RNN)
            else:
                o, h_new, conv_new = rglru_prefill(y.reshape(b, seq, 2 * D_RNN), rg_h0[j], rg_conv0[j], *rg_args,
                                                   start=start, tl=min(256, seq))
                o, h_new = o.reshape(t, D_RNN), h_new.reshape(b, D_RNN)
            rg_h_new.append(h_new)
            rg_conv_new.append(conv_new)
            xt = out_proj(o, w["rg_out"][j], xt, g1, **kw)
        hm, q = norm_proj(xt, p["norm_ffn_g"][i], sc2, sh2, w["peer_q"][i], want_h=True, **kw)
        delta = peer_mix(hm, q, w["peer_keys"][i], w["peer_u"][i], w["peer_v"][i], tm=min(256, t))
        gate = g2
    zeros = jnp.zeros_like(gate)
    xt, y = norm_proj(xt, p["final_norm_g"], zeros, zeros, delta=delta, gate=gate, want_h=True, seq_len=seq, tm=tm)
    return y.reshape(b, seq, D_MODEL), jnp.stack(gla_new), jnp.stack(rg_h_new), jnp.stack(rg_conv_new)


def kernel(x_prompt, x_sample, c_prompt, c_sample, state_gla, state_rglru_h, state_rglru_conv, ada_w, ada_b, norm_mix_g, norm_ffn_g, gla_w_in, gla_w_gate2, gla_b_gate, gla_onorm_g, gla_w_out, rg_w_in, rg_conv_w, rg_conv_b, rg_w_a, rg_b_a, rg_w_x, rg_b_x, rg_lambda, rg_w_out, peer_w_q, peer_sub_keys, peer_u, peer_v, final_norm_g):
    p = dict(norm_mix_g=norm_mix_g, norm_ffn_g=norm_ffn_g, gla_w_in=gla_w_in, gla_w_gate2=gla_w_gate2,
             gla_b_gate=gla_b_gate, gla_onorm_g=gla_onorm_g, gla_w_out=gla_w_out, rg_w_in=rg_w_in,
             rg_conv_w=rg_conv_w, rg_conv_b=rg_conv_b, rg_w_a=rg_w_a, rg_b_a=rg_b_a, rg_w_x=rg_w_x,
             rg_b_x=rg_b_x, rg_lambda=rg_lambda, rg_w_out=rg_w_out, peer_w_q=peer_w_q,
             peer_sub_keys=peer_sub_keys, peer_u=peer_u, peer_v=peer_v, final_norm_g=final_norm_g)
    w = _prep_weights(p)
    nb_p, nb_s = x_prompt.shape[0], x_sample.shape[0]
    n_gla, n_rg = state_gla.shape[0], state_rglru_h.shape[0]
    mod = adaln(jnp.concatenate([c_prompt, c_sample], axis=0), ada_w, ada_b)
    gla0 = jnp.zeros((n_gla, nb_p) + state_gla.shape[2:], f32)
    h0 = jnp.zeros((n_rg, nb_p, D_RNN), f32)
    conv0 = jnp.zeros((n_rg, nb_p, CONV_W - 1, D_RNN), f32)
    per = nb_p // PROMPT_STREAMS
    parts = []
    for s in range(PROMPT_STREAMS):
        rows = slice(s * per, (s + 1) * per)
        parts.append(_trunk(x_prompt[rows], mod[:, rows], 0, gla0[:, rows], h0[:, rows], conv0[:, rows], p, w,
                            decode=False))
    y_p = jnp.concatenate([o[0] for o in parts], axis=0)
    gla_p, h_p, conv_p = (jnp.concatenate([o[k] for o in parts], axis=1) for k in (1, 2, 3))
    y_s, gla_s, h_s, conv_s = _trunk(x_sample, mod[:, nb_p:], PAST_LEN, state_gla, state_rglru_h, state_rglru_conv,
                                     p, w, decode=True)
    return (y_p, y_s, gla_p, gla_s, h_p, h_s, conv_p, conv_s)
```

```python
import functools

import jax
import jax.numpy as jnp
from jax import lax
from jax.experimental import pallas as pl
from jax.experimental.pallas import tpu as pltpu
from jax.experimental.pallas import tpu_sc as plsc

f32 = jnp.float32
bf16 = jnp.bfloat16
HIGHEST = lax.Precision.HIGHEST

D_MODEL = 1024
DEPTH = 4
GLA_HEADS = 4
GLA_DK = 512
GLA_DV = 1024
GLA_HDK = 128
GLA_HDV = 256
GLA_RANK = 16
GLA_TAU = 16.0
GLA_CHUNK = 64
D_RNN = 1280
RG_BLOCKS = 8
RG_BLOCK = 160
CONV_W = 4
RG_C = 8.0
PEER_HEADS = 8
PEER_NKEYS = 128
PEER_DHALF = 128
PEER_TOPK = 16
PEER_ROWS = PEER_HEADS * PEER_TOPK
PAST_LEN = 16384
EPS = 1e-6

SC_CORES = 2
SC_SUBCORES = 16
SC_LANES = 16
SC_WORKERS = SC_CORES * SC_SUBCORES
LANES = 128
HALF_W = D_MODEL // 2
SC_CHUNK_ROWS = 64
SC_TOK_BLOCK = 16
PROMPT_STREAMS = 4


def _rms(x, gain):
    return x * lax.rsqrt(jnp.mean(x * x, axis=-1, keepdims=True) + EPS) * gain


def _bf16_bits(v):
    return lax.bitcast_convert_type(v.astype(bf16).astype(f32), jnp.int32)


def _pack_halves(v):
    u = _bf16_bits(v)
    half = v.shape[-1] // 2
    return (u[:, half:] & jnp.int32(-65536)) | lax.shift_right_logical(u[:, :half], 16)


def _pack_twice(v):
    u = _bf16_bits(v)
    return u | lax.shift_right_logical(u, 16)


def _adaln_kernel(c_ref, w_ref, b_ref, o_ref):
    cond = jax.nn.silu(c_ref[...])
    o_ref[0] = jnp.dot(cond, w_ref[0], precision=HIGHEST, preferred_element_type=f32) + b_ref[0]


def adaln(c, ada_w, ada_b):
    bc = c.shape[0]
    tn = 1024
    return pl.pallas_call(
        _adaln_kernel,
        grid=(DEPTH, 6 * D_MODEL // tn),
        in_specs=[
            pl.BlockSpec((bc, D_MODEL), lambda i, j: (0, 0)),
            pl.BlockSpec((1, D_MODEL, tn), lambda i, j: (i, 0, j)),
            pl.BlockSpec((1, 1, tn), lambda i, j: (i, 0, j)),
        ],
        out_specs=pl.BlockSpec((1, bc, tn), lambda i, j: (i, 0, j)),
        out_shape=jax.ShapeDtypeStruct((DEPTH, bc, 6 * D_MODEL), f32),
        name="adaln",
    )(c, ada_w, ada_b.reshape(DEPTH, 1, 6 * D_MODEL))


def _proj_kernel(*refs, has_delta, has_w, want_h):
    it = iter(refs)
    x_ref = next(it)
    delta_ref = next(it) if has_delta else None
    gate_ref = next(it) if has_delta else None
    g_ref, sc_ref, sh_ref = next(it), next(it), next(it)
    w_ref = next(it) if has_w else None
    xo_ref = next(it) if has_delta else None
    h_ref = next(it) if want_h else None
    y_ref = next(it) if has_w else None
    hb_ref = next(it) if has_w else None

    def prologue():
        x = x_ref[...]
        if has_delta:
            x = x + gate_ref[...].reshape(-1, D_MODEL) * delta_ref[...]
            xo_ref[...] = x
        h = _rms(x, g_ref[...])
        h = h * (1.0 + sc_ref[...].reshape(-1, D_MODEL)) + sh_ref[...].reshape(-1, D_MODEL)
        if want_h == "packed":
            h_ref[...] = _pack_halves(h)
        elif want_h:
            h_ref[...] = h
        if has_w:
            hb_ref[...] = h.astype(bf16)

    if has_w:
        pl.when(pl.program_id(1) == 0)(prologue)
        y_ref[...] = jnp.dot(hb_ref[...], w_ref[...], preferred_element_type=f32)
    else:
        prologue()


def _mod_spec(mod, tm, seq_len):
    if mod.ndim == 3:
        per = seq_len // tm
        return pl.BlockSpec((1, 1, D_MODEL), lambda i, j: (i // per, 0, 0))
    return pl.BlockSpec((tm, D_MODEL), lambda i, j: (i, 0))


def norm_proj(x, gain, scale, shift, w=None, *, delta=None, gate=None, want_h=False, seq_len, tm, tn=512):
    t = x.shape[0]
    has_delta, has_w = delta is not None, w is not None
    n = w.shape[1] if has_w else 0
    tn = min(tn, n) if has_w else 0
    grid = (t // tm, n // tn if has_w else 1)
    row = pl.BlockSpec((tm, D_MODEL), lambda i, j: (i, 0))
    args, specs = [x], [row]
    if has_delta:
        args += [delta, gate]
        specs += [row, _mod_spec(gate, tm, seq_len)]
    args += [gain.reshape(1, D_MODEL), scale, shift]
    specs += [pl.BlockSpec((1, D_MODEL), lambda i, j: (0, 0)), _mod_spec(scale, tm, seq_len),
              _mod_spec(shift, tm, seq_len)]
    out_shapes, out_specs, scratch = [], [], []
    if has_delta:
        out_shapes.append(jax.ShapeDtypeStruct((t, D_MODEL), f32))
        out_specs.append(row)
    if want_h == "packed":
        out_shapes.append(jax.ShapeDtypeStruct((t, HALF_W), jnp.int32))
        out_specs.append(pl.BlockSpec((tm, HALF_W), lambda i, j: (i, 0)))
    elif want_h:
        out_shapes.append(jax.ShapeDtypeStruct((t, D_MODEL), f32))
        out_specs.append(row)
    if has_w:
        args.append(w)
        specs.append(pl.BlockSpec((D_MODEL, tn), lambda i, j: (0, j)))
        out_shapes.append(jax.ShapeDtypeStruct((t, n), f32))
        out_specs.append(pl.BlockSpec((tm, tn), lambda i, j: (i, j)))
        scratch.append(pltpu.VMEM((tm, D_MODEL), bf16))
    return pl.pallas_call(
        functools.partial(_proj_kernel, has_delta=has_delta, has_w=has_w, want_h=want_h),
        grid=grid, in_specs=specs, out_specs=out_specs, out_shape=out_shapes,
        scratch_shapes=scratch,
        compiler_params=pltpu.CompilerParams(dimension_semantics=("parallel", "arbitrary")),
        name="norm_proj",
    )(*args)


def _out_proj_kernel(a_ref, w_ref, x_ref, gate_ref, o_ref):
    y = jnp.dot(a_ref[...], w_ref[...], preferred_element_type=f32)
    o_ref[...] = x_ref[...] + gate_ref[...].reshape(-1, D_MODEL) * y


def out_proj(a, w, x, gate, *, seq_len, tm):
    t, k = a.shape
    row = pl.BlockSpec((tm, D_MODEL), lambda i, j: (i, 0))
    return pl.pallas_call(
        _out_proj_kernel,
        grid=(t // tm, 1),
        in_specs=[pl.BlockSpec((tm, k), lambda i, j: (i, 0)),
                  pl.BlockSpec((k, D_MODEL), lambda i, j: (0, 0)),
                  row, _mod_spec(gate, tm, seq_len)],
        out_specs=row,
        out_shape=jax.ShapeDtypeStruct((t, D_MODEL), f32),
        name="out_proj",
    )(a, w, x, gate)


def _gla_kernel(q_ref, k_ref, v_ref, g_ref, lr_ref, wg_ref, bg_ref, on_ref, s0_ref,
                o_ref, so_ref, s_scr, *, bb, chunk, valid):
    c = pl.program_id(1)

    @pl.when(c == 0)
    def _():
        s_scr[...] = s0_ref[...]

    rows = lax.broadcasted_iota(jnp.int32, (chunk, chunk), 0)
    cols = lax.broadcasted_iota(jnp.int32, (chunk, chunk), 1)
    causal = rows >= cols
    tri = causal.astype(f32)
    ones = jnp.ones((chunk, GLA_HDV), f32)
    nt = (((1,), (1,)), ((), ()))
    tn = (((0,), (0,)), ((), ()))
    for bi in range(bb):
        gate_in = jnp.dot(lr_ref[bi].astype(bf16), wg_ref[...], preferred_element_type=f32) + bg_ref[...]
        log_a = jax.nn.log_sigmoid(gate_in) / GLA_TAU
        if valid < chunk:
            log_a = jnp.where(lax.broadcasted_iota(jnp.int32, log_a.shape, 0) < valid, log_a, 0.0)
        for h in range(GLA_HEADS):
            ks = slice(h * GLA_HDK, (h + 1) * GLA_HDK)
            vs = slice(h * GLA_HDV, (h + 1) * GLA_HDV)
            la = log_a[:, ks]
            b = jnp.dot(tri, la, precision=HIGHEST, preferred_element_type=f32)
            q = q_ref[bi, :, ks] * (GLA_HDK ** -0.5)
            k = k_ref[bi, :, ks]
            v = v_ref[bi, :, vs].astype(bf16)
            qd = (q * jnp.exp(b)).astype(bf16)
            kd = (k * jnp.exp(-b)).astype(bf16)
            scores = lax.dot_general(qd, kd, nt, preferred_element_type=f32)
            scores = jnp.where(causal, scores, 0.0).astype(bf16)
            s_old = s_scr[bi, h]
            o = (jnp.dot(scores, v, preferred_element_type=f32)
                 + jnp.dot(qd, s_old.astype(bf16), preferred_element_type=f32))
            b_end = b[chunk - 1:chunk, :]
            k_end = (k * jnp.exp(b_end - b)).astype(bf16)
            decay = jnp.exp(lax.dot_general(la, ones, tn, precision=HIGHEST, preferred_element_type=f32))
            s_scr[bi, h] = s_old * decay + lax.dot_general(k_end, v, tn, preferred_element_type=f32)
            on = _rms(o, on_ref[...])
            o_ref[bi, :, vs] = (on * jax.nn.silu(g_ref[bi, :, vs])).astype(o_ref.dtype)

    @pl.when(c == pl.num_programs(1) - 1)
    def _():
        so_ref[...] = s_scr[...]


def gla_mix(y, lr, w_gate2p, b_gate, onorm_g, s0, *, bb, chunk, valid):
    b, seq, _ = y.shape
    grid = (b // bb, seq // chunk)

    def col(width, blk):
        return pl.BlockSpec((bb, chunk, width), lambda i, c: (i, c, blk))

    st = pl.BlockSpec((bb, GLA_HEADS, GLA_HDK, GLA_HDV), lambda i, c: (i, 0, 0, 0))
    return pl.pallas_call(
        functools.partial(_gla_kernel, bb=bb, chunk=chunk, valid=valid),
        grid=grid,
        in_specs=[col(GLA_DK, 0), col(GLA_DK, 1), col(GLA_DV, 1), col(GLA_DV, 2),
                  pl.BlockSpec((bb, chunk, LANES), lambda i, c: (i, c, 0)),
                  pl.BlockSpec((LANES, GLA_DK), lambda i, c: (0, 0)),
                  pl.BlockSpec((1, GLA_DK), lambda i, c: (0, 0)),
                  pl.BlockSpec((1, GLA_HDV), lambda i, c: (0, 0)),
                  st],
        out_specs=[pl.BlockSpec((bb, chunk, GLA_DV), lambda i, c: (i, c, 0)), st],
        out_shape=[jax.ShapeDtypeStruct((b, seq, GLA_DV), bf16),
                   jax.ShapeDtypeStruct(s0.shape, f32)],
        scratch_shapes=[pltpu.VMEM((bb, GLA_HEADS, GLA_HDK, GLA_HDV), f32)],
        compiler_params=pltpu.CompilerParams(dimension_semantics=("parallel", "arbitrary")),
        name="gla_mix",
    )(y, y, y, y, lr, w_gate2p, b_gate.reshape(1, GLA_DK), onorm_g.reshape(1, GLA_HDV), s0)


def _rg_gates(xconv, wa_ref, ba_ref, wx_ref, bx_ref, lam_ref):
    xb = xconv.astype(bf16)
    gate_r = jax.nn.sigmoid(jnp.dot(xb, wa_ref[...], preferred_element_type=f32) + ba_ref[...])
    gate_i = jax.nn.sigmoid(jnp.dot(xb, wx_ref[...], preferred_element_type=f32) + bx_ref[...])
    log_a = -RG_C * gate_r * jax.nn.softplus(-lam_ref[...])
    a = jnp.exp(log_a)
    mult = jnp.sqrt(-jnp.tanh(log_a) * (a * a + 1.0))
    return a, mult, gate_i


def _rg_prefill_kernel(y_ref, xb_ref, h0_ref, c0_ref, cw_ref, cb_ref, wa_ref, ba_ref, wx_ref, bx_ref,
                       lam_ref, o_ref, ho_ref, co_ref, buf, h_scr, *, tl, start):
    t = pl.program_id(1)
    top = 8

    @pl.when(t == 0)
    def _():
        h_scr[...] = h0_ref[0]
        buf[top - (CONV_W - 1):top, :] = c0_ref[0]

    x = xb_ref[0]
    buf[top:top + tl, :] = x
    xconv = cb_ref[...] + sum(buf[top - (CONV_W - 1) + j:top - (CONV_W - 1) + j + tl, :] * cw_ref[j:j + 1, :]
                              for j in range(CONV_W))
    a, mult, gate_i = _rg_gates(xconv, wa_ref, ba_ref, wx_ref, bx_ref, lam_ref)
    row = lax.broadcasted_iota(jnp.int32, (tl, D_RNN), 0)
    reset = (row + t * tl + start) == 0
    a = jnp.where(reset, 0.0, a)
    mult = jnp.where(reset, 1.0, mult)
    b = mult * gate_i * xconv
    s = 1
    while s < tl:
        keep = row >= s
        b = jnp.where(keep, a * pltpu.roll(b, s, 0) + b, b)
        a = jnp.where(keep, a * pltpu.roll(a, s, 0), a)
        s *= 2
    hs = a * h_scr[...] + b
    o_ref[0] = (hs * jax.nn.gelu(y_ref[0])).astype(o_ref.dtype)
    h_scr[...] = hs[tl - 1:tl, :]
    buf[top - (CONV_W - 1):top, :] = x[tl - (CONV_W - 1):tl, :]

    @pl.when(t == pl.num_programs(1) - 1)
    def _():
        ho_ref[0] = hs[tl - 1:tl, :]
        co_ref[0] = x[tl - (CONV_W - 1):tl, :]


def _rg_vec(v):
    return v.reshape(1, D_RNN)


def rglru_prefill(y, h0, conv0, conv_w, conv_b, wa, ba, wx, bx, lam, *, start, tl):
    b, seq, _ = y.shape
    assert seq % tl == 0 and tl >= CONV_W - 1
    vec = pl.BlockSpec((1, D_RNN), lambda i, t: (0, 0))
    mat = pl.BlockSpec((D_RNN, D_RNN), lambda i, t: (0, 0))
    return pl.pallas_call(
        functools.partial(_rg_prefill_kernel, tl=tl, start=start),
        grid=(b, seq // tl),
        in_specs=[pl.BlockSpec((1, tl, D_RNN), lambda i, t: (i, t, 0)),
                  pl.BlockSpec((1, tl, D_RNN), lambda i, t: (i, t, 1)),
                  pl.BlockSpec((1, 1, D_RNN), lambda i, t: (i, 0, 0)),
                  pl.BlockSpec((1, CONV_W - 1, D_RNN), lambda i, t: (i, 0, 0)),
                  pl.BlockSpec((CONV_W, D_RNN), lambda i, t: (0, 0)),
                  vec, mat, vec, mat, vec, vec],
        out_specs=[pl.BlockSpec((1, tl, D_RNN), lambda i, t: (i, t, 0)),
                   pl.BlockSpec((1, 1, D_RNN), lambda i, t: (i, 0, 0)),
                   pl.BlockSpec((1, CONV_W - 1, D_RNN), lambda i, t: (i, 0, 0))],
        out_shape=[jax.ShapeDtypeStruct((b, seq, D_RNN), bf16),
                   jax.ShapeDtypeStruct((b, 1, D_RNN), f32),
                   jax.ShapeDtypeStruct((b, CONV_W - 1, D_RNN), f32)],
        scratch_shapes=[pltpu.VMEM((8 + tl, D_RNN), f32), pltpu.VMEM((1, D_RNN), f32)],
        compiler_params=pltpu.CompilerParams(dimension_semantics=("parallel", "arbitrary")),
        name="rglru_prefill",
    )(y, y, h0.reshape(b, 1, D_RNN), conv0, conv_w, _rg_vec(conv_b), wa, _rg_vec(ba), wx, _rg_vec(bx),
      _rg_vec(lam))


def _rg_decode_kernel(y_ref, xb_ref, h0_ref, c0_ref, cw_ref, cb_ref, wa_ref, ba_ref, wx_ref, bx_ref,
                      lam_ref, o_ref, ho_ref, *, steps, start):
    xc = [c0_ref[j] for j in range(CONV_W - 1)] + [xb_ref[s] for s in range(steps)]
    h = h0_ref[...]
    for s in range(steps):
        xconv = cb_ref[...] + sum(xc[s + j] * cw_ref[j:j + 1, :] for j in range(CONV_W))
        a, mult, gate_i = _rg_gates(xconv, wa_ref, ba_ref, wx_ref, bx_ref, lam_ref)
        if start + s == 0:
            a, mult = 0.0, 1.0
        h = a * h + mult * gate_i * xconv
        o_ref[s] = (h * jax.nn.gelu(y_ref[s])).astype(o_ref.dtype)
    ho_ref[...] = h


def rglru_decode(y, h0, conv0, conv_w, conv_b, wa, ba, wx, bx, lam, *, start):
    steps, b, _ = y.shape
    whole = lambda *shape: pl.BlockSpec(shape, lambda i: (0,) * len(shape))
    vec, mat = whole(1, D_RNN), whole(D_RNN, D_RNN)
    return pl.pallas_call(
        functools.partial(_rg_decode_kernel, steps=steps, start=start),
        grid=(1,),
        in_specs=[pl.BlockSpec((steps, b, D_RNN), lambda i: (0, 0, 0)),
                  pl.BlockSpec((steps, b, D_RNN), lambda i: (0, 0, 1)),
                  whole(b, D_RNN), whole(CONV_W - 1, b, D_RNN), whole(CONV_W, D_RNN),
                  vec, mat, vec, mat, vec, vec],
        out_specs=[whole(steps, b, D_RNN), whole(b, D_RNN)],
        out_shape=[jax.ShapeDtypeStruct((steps, b, D_RNN), bf16),
                   jax.ShapeDtypeStruct((b, D_RNN), f32)],
        name="rglru_decode",
    )(y, y, h0, conv0, conv_w, _rg_vec(conv_b), wa, _rg_vec(ba), wx, _rg_vec(bx), _rg_vec(lam))


def _top_rows(v, nrows, ntok):
    rowi = lax.broadcasted_iota(jnp.int32, (nrows, ntok), 0)
    vals, idxs = [], []
    for _ in range(PEER_TOPK):
        m = jnp.max(v, axis=0, keepdims=True)
        i = jnp.min(jnp.where(v == m, rowi, nrows), axis=0, keepdims=True)
        vals.append(m)
        idxs.append(i)
        v = jnp.where(rowi == i, -jnp.inf, v)
    return jnp.concatenate(vals, axis=0), jnp.concatenate(idxs, axis=0)


def _peer_topk_kernel(q_ref, keys_ref, idx_ref, g_ref, *, tm):
    nt = (((1,), (1,)), ((), ()))
    idx_rows, g_rows = [], []
    for h in range(PEER_HEADS):
        top = []
        for p in range(2):
            col = (h * 2 + p) * PEER_DHALF
            qs = q_ref[:, col:col + PEER_DHALF].astype(bf16)
            s = lax.dot_general(keys_ref[h, p], qs, nt, preferred_element_type=f32)
            top.append(_top_rows(s, PEER_NKEYS, tm))
        (s0, i0), (s1, i1) = top
        cand = jnp.concatenate([s0[a:a + 1] + s1 for a in range(PEER_TOPK)], axis=0)
        cidx = jnp.concatenate([i0[a:a + 1] * PEER_NKEYS + i1 for a in range(PEER_TOPK)], axis=0)
        best, pos = _top_rows(cand, PEER_TOPK * PEER_TOPK, tm)
        rowi = lax.broadcasted_iota(jnp.int32, (PEER_TOPK * PEER_TOPK, tm), 0)
        idx = jnp.concatenate(
            [jnp.max(jnp.where(rowi == pos[k:k + 1], cidx, -1), axis=0, keepdims=True)
             for k in range(PEER_TOPK)], axis=0)
        e = jnp.exp(best - best[0:1])
        g_rows.append(e / jnp.sum(e, axis=0, keepdims=True))
        idx_rows.append(idx.astype(f32))
    idx_ref[...] = jnp.concatenate(idx_rows, axis=0).T.astype(jnp.int32)
    g_ref[...] = jnp.concatenate(g_rows, axis=0).T


def peer_topk(q, keys, *, tm):
    t = q.shape[0]
    out = pl.BlockSpec((tm, PEER_ROWS), lambda i: (i, 0))
    return pl.pallas_call(
        functools.partial(_peer_topk_kernel, tm=tm),
        grid=(t // tm,),
        in_specs=[pl.BlockSpec((tm, 2 * PEER_HEADS * PEER_DHALF), lambda i: (i, 0)),
                  pl.BlockSpec((PEER_HEADS, 2, PEER_NKEYS, PEER_DHALF), lambda i: (0, 0, 0, 0))],
        out_specs=[out, out],
        out_shape=[jax.ShapeDtypeStruct((t, PEER_ROWS), jnp.int32),
                   jax.ShapeDtypeStruct((t, PEER_ROWS), f32)],
        name="peer_topk",
    )(q, keys)


def _peer_mid_kernel(dots_ref, g_ref, o_ref):
    o_ref[...] = _pack_twice(g_ref[...] * jax.nn.gelu(dots_ref[...]))


def peer_mid(dots, g, *, tm):
    t = dots.shape[0]
    blk = pl.BlockSpec((tm, PEER_ROWS), lambda i: (i, 0))
    return pl.pallas_call(
        _peer_mid_kernel,
        grid=(t // tm,),
        in_specs=[blk, blk],
        out_specs=blk,
        out_shape=jax.ShapeDtypeStruct((t, PEER_ROWS), jnp.int32),
        name="peer_mid",
    )(dots, g)


def _as_pairs(w):
    return plsc.bitcast(w, bf16)


def _sum4_split(p):
    s = plsc.bitcast((p[0] + p[1]) + (p[2] + p[3]), jnp.int32)
    return plsc.bitcast(s << 16, f32), plsc.bitcast(s & jnp.int32(-65536), f32)


def _sc_worker_base(tokens_per_worker):
    wid = lax.axis_index("s") * SC_CORES + lax.axis_index("c")
    return wid * tokens_per_worker


def _sc_kernel(body, t, out_width, scratch):
    mesh = plsc.VectorSubcoreMesh(core_axis_name="c", subcore_axis_name="s")
    return pl.kernel(
        body, mesh=mesh,
        out_type=jax.ShapeDtypeStruct((t, out_width), f32),
        scratch_types=scratch,
        compiler_params=pltpu.CompilerParams(needs_layout_passes=False),
    )


def _sc_gather_loop(tab_hbm, idx_v, rows_v, sems, compute):
    def gather(tok, half):
        return pltpu.make_async_copy(tab_hbm.at[idx_v.at[tok, half]], rows_v.at[half], sems.at[half])

    gather(0, 0).start()

    def token(tok, carry):
        gather(tok, 1).start()
        gather(tok, 0).wait()
        compute(tok, 0, rows_v.at[0])

        @pl.when(tok + 1 < SC_TOK_BLOCK)
        def _():
            gather(tok + 1, 0).start()

        gather(tok, 1).wait()
        compute(tok, 1, rows_v.at[1])
        return carry

    lax.fori_loop(0, SC_TOK_BLOCK, token, 0)


def sc_peer_dots(x, idx, tab):
    t = x.shape[0]
    tpw = t // SC_WORKERS
    rpg = 8
    quads = HALF_W // SC_LANES // 4

    def body(x_hbm, idx_hbm, tab_hbm, out_hbm, x_v, idx_v, rows_v, part_v, out_v, sems):
        base = _sc_worker_base(tpw)
        lane = lax.iota(jnp.int32, SC_LANES)

        def block(blk, carry):
            t0 = base + blk * SC_TOK_BLOCK
            pltpu.sync_copy(x_hbm.at[pl.ds(t0, SC_TOK_BLOCK)], x_v)
            pltpu.sync_copy(idx_hbm.at[pl.ds(t0, SC_TOK_BLOCK)], idx_v)

            def compute(tok, half, rows):
                def group(g, c2):
                    def words(j4, accs):
                        cols = [pl.ds((j4 * 4 + q) * SC_LANES, SC_LANES) for q in range(4)]
                        xs = [_as_pairs(x_v[tok, c]) for c in cols]
                        new = []
                        for r in range(rpg):
                            lo, hi = _sum4_split([_as_pairs(rows[g * rpg + r, c]) * xq for c, xq in zip(cols, xs)])
                            new.append(accs[r] + (lo + hi))
                        return tuple(new)

                    accs = lax.fori_loop(0, quads, words, tuple(jnp.zeros((SC_LANES,), f32) for _ in range(rpg)))
                    for r in range(rpg):
                        part_v[pl.ds((g * rpg + r) * SC_LANES, SC_LANES)] = accs[r]
                    return c2

                lax.fori_loop(0, SC_CHUNK_ROWS // rpg, group, 0)

                def fold(rb, c3):
                    first = rb * (SC_LANES * SC_LANES) + lane * SC_LANES
                    s = plsc.load_gather(part_v, [first])
                    for c in range(1, SC_LANES):
                        s = s + plsc.load_gather(part_v, [first + c])
                    out_v[tok, pl.ds(half * SC_CHUNK_ROWS + rb * SC_LANES, SC_LANES)] = s
                    return c3

                lax.fori_loop(0, SC_CHUNK_ROWS // SC_LANES, fold, 0)

            _sc_gather_loop(tab_hbm, idx_v, rows_v, sems, compute)
            pltpu.sync_copy(out_v, out_hbm.at[pl.ds(t0, SC_TOK_BLOCK)])
            return carry

        lax.fori_loop(0, tpw // SC_TOK_BLOCK, block, 0)

    scratch = [pltpu.VMEM((SC_TOK_BLOCK, HALF_W), jnp.int32),
               pltpu.VMEM((SC_TOK_BLOCK, 2, SC_CHUNK_ROWS), jnp.int32),
               pltpu.VMEM((2, SC_CHUNK_ROWS, HALF_W), jnp.int32),
               pltpu.VMEM((SC_CHUNK_ROWS * SC_LANES,), f32),
               pltpu.VMEM((SC_TOK_BLOCK, PEER_ROWS), f32),
               pltpu.SemaphoreType.DMA((2,))]
    return _sc_kernel(body, t, PEER_ROWS, scratch)(x, idx, tab)


def sc_peer_mix(wgt, idx, tab):
    t = wgt.shape[0]
    tpw = t // SC_WORKERS
    wpg = 8
    ngroups = HALF_W // SC_LANES // wpg

    def body(w_hbm, idx_hbm, tab_hbm, out_hbm, w_v, idx_v, rows_v, out_v, sems):
        base = _sc_worker_base(tpw)
        zero = jnp.zeros((SC_LANES,), jnp.int32)

        def block(blk, carry):
            t0 = base + blk * SC_TOK_BLOCK
            pltpu.sync_copy(w_hbm.at[pl.ds(t0, SC_TOK_BLOCK)], w_v)
            pltpu.sync_copy(idx_hbm.at[pl.ds(t0, SC_TOK_BLOCK)], idx_v)

            def compute(tok, half, rows):
                def group(g, c2):
                    def out_at(jj, hw):
                        return out_v.at[tok, pl.ds(hw * HALF_W + (g * wpg + jj) * SC_LANES, SC_LANES)]

                    def quad(k4, accs):
                        wk = [_as_pairs(plsc.load_gather(w_v.at[tok], [zero + (half * SC_CHUNK_ROWS + k4 * 4 + q)]))
                              for q in range(4)]
                        new = []
                        for jj in range(wpg):
                            col = pl.ds((g * wpg + jj) * SC_LANES, SC_LANES)
                            lo, hi = _sum4_split([_as_pairs(rows[k4 * 4 + q, col]) * wk[q] for q in range(4)])
                            new.append(accs[2 * jj] + lo)
                            new.append(accs[2 * jj + 1] + hi)
                        return tuple(new)

                    if half == 0:
                        init = tuple(jnp.zeros((SC_LANES,), f32) for _ in range(2 * wpg))
                    else:
                        init = tuple(out_at(jj, hw)[...] for jj in range(wpg) for hw in range(2))
                    accs = lax.fori_loop(0, SC_CHUNK_ROWS // 4, quad, init)
                    for jj in range(wpg):
                        for hw in range(2):
                            out_at(jj, hw)[...] = accs[2 * jj + hw]
                    return c2

                lax.fori_loop(0, ngroups, group, 0)

            _sc_gather_loop(tab_hbm, idx_v, rows_v, sems, compute)
            pltpu.sync_copy(out_v, out_hbm.at[pl.ds(t0, SC_TOK_BLOCK)])
            return carry

        lax.fori_loop(0, tpw // SC_TOK_BLOCK, block, 0)

    scratch = [pltpu.VMEM((SC_TOK_BLOCK, PEER_ROWS), jnp.int32),
               pltpu.VMEM((SC_TOK_BLOCK, 2, SC_CHUNK_ROWS), jnp.int32),
               pltpu.VMEM((2, SC_CHUNK_ROWS, HALF_W), jnp.int32),
               pltpu.VMEM((SC_TOK_BLOCK, D_MODEL), f32),
               pltpu.SemaphoreType.DMA((2,))]
    return _sc_kernel(body, t, D_MODEL, scratch)(wgt, idx, tab)


def pack_table(tab):
    b = lax.bitcast_convert_type(tab.astype(bf16), jnp.uint16).astype(jnp.uint32)
    return lax.bitcast_convert_type(b[:, :HALF_W] | (b[:, HALF_W:] << 16), jnp.int32)


def peer_mix(hm, q, keys, u_packed, v_packed, *, tm):
    t = hm.shape[0]
    idx, g = peer_topk(q, keys, tm=tm)
    idx = idx.reshape(t, 2, SC_CHUNK_ROWS)
    dots = sc_peer_dots(hm, idx, u_packed)
    wgt = peer_mid(dots, g, tm=tm)
    return sc_peer_mix(wgt, idx, v_packed)


def _block_diag(w):
    nb, n, _ = w.shape
    eye = jnp.eye(nb, dtype=w.dtype)
    return (eye[:, None, :, None] * w[:, :, None, :]).reshape(nb * n, nb * n)


def _prep_weights(p):
    w = {}
    w["gla_main"] = p["gla_w_in"][:, :, :2 * GLA_DK + 2 * GLA_DV].astype(bf16)
    w["gla_lr"] = jnp.pad(p["gla_w_in"][:, :, 2 * GLA_DK + 2 * GLA_DV:],
                          ((0, 0), (0, 0), (0, LANES - GLA_RANK))).astype(bf16)
    w["gla_gate2"] = jnp.pad(p["gla_w_gate2"], ((0, 0), (0, LANES - GLA_RANK), (0, 0))).astype(bf16)
    w["gla_out"] = p["gla_w_out"].astype(bf16)
    w["rg_in"] = p["rg_w_in"].astype(bf16)
    w["rg_wa"] = jax.vmap(_block_diag)(p["rg_w_a"]).astype(bf16)
    w["rg_wx"] = jax.vmap(_block_diag)(p["rg_w_x"]).astype(bf16)
    w["rg_out"] = p["rg_w_out"].astype(bf16)
    w["peer_q"] = p["peer_w_q"].astype(bf16)
    w["peer_keys"] = p["peer_sub_keys"].astype(bf16)
    w["peer_u"] = jax.vmap(pack_table)(p["peer_u"])
    w["peer_v"] = jax.vmap(pack_table)(p["peer_v"])
    return w


def _trunk(x, mod, start, gla_s0, rg_h0, rg_conv0, p, w, *, decode):
    b, seq, _ = x.shape
    t = b * seq
    tm = min(512, t if decode else seq)
    xt = x.reshape(t, D_MODEL)
    if decode:
        chunk = 16
        bb = 4
    else:
        chunk = min(GLA_CHUNK, seq)
        bb = 1
    delta = gate = None
    gla_new, rg_h_new, rg_conv_new = [], [], []
    for i in range(DEPTH):
        m = mod[i].reshape(b, 6, D_MODEL)
        if decode:
            mods = [jnp.repeat(m[:, j], seq, axis=0) for j in range(6)]
        else:
            mods = [m[:, j:j + 1] for j in range(6)]
        sh1, sc1, g1, sh2, sc2, g2 = mods
        kw = dict(seq_len=seq, tm=tm)
        j = i // 2
        if i % 2 == 0:
            if delta is None:
                (y,) = norm_proj(xt, p["norm_mix_g"][i], sc1, sh1, w["gla_main"][j], **kw)
            else:
                xt, y = norm_proj(xt, p["norm_mix_g"][i], sc1, sh1, w["gla_main"][j], delta=delta, gate=gate, **kw)
            (lr,) = norm_proj(xt, p["norm_mix_g"][i], sc1, sh1, w["gla_lr"][j], **kw)
            y3, lr3 = y.reshape(b, seq, -1), lr.reshape(b, seq, LANES)
            if decode:
                padding = ((0, 0), (0, chunk - seq), (0, 0))
                y3, lr3 = jnp.pad(y3, padding), jnp.pad(lr3, padding)
            o, s_new = gla_mix(y3, lr3, w["gla_gate2"][j], p["gla_b_gate"][j], p["gla_onorm_g"][j],
                               gla_s0[j], bb=bb, chunk=chunk, valid=min(seq, chunk))
            gla_new.append(s_new)
            o = o[:, :seq].reshape(t, GLA_DV)
            xt = out_proj(o, w["gla_out"][j], xt, g1, **kw)
        else:
            xt, y = norm_proj(xt, p["norm_mix_g"][i], sc1, sh1, w["rg_in"][j], delta=delta, gate=gate, **kw)
            rg_args = (p["rg_conv_w"][j], p["rg_conv_b"][j], w["rg_wa"][j], p["rg_b_a"][j], w["rg_wx"][j],
                       p["rg_b_x"][j], p["rg_lambda"][j])
            if decode:
                y3 = y.reshape(b, seq, 2 * D_RNN).transpose(1, 0, 2)
                o, h_new = rglru_decode(y3, rg_h0[j], rg_conv0[j].transpose(1, 0, 2), *rg_args, start=start)
                xbr = y.reshape(b, seq, 2 * D_RNN)[:, :, D_RNN:]
                conv_new = jnp.concatenate([rg_conv0[j], xbr], axis=1)[:, seq:]
                o = o.transpose(1, 0, 2).reshape(t, D_RNN)
            else:
                o, h_new, conv_new = rglru_prefill(y.reshape(b, seq, 2 * D_RNN), rg_h0[j], rg_conv0[j], *rg_args,
                                                   start=start, tl=min(256, seq))
                o, h_new = o.reshape(t, D_RNN), h_new.reshape(b, D_RNN)
            rg_h_new.append(h_new)
            rg_conv_new.append(conv_new)
            xt = out_proj(o, w["rg_out"][j], xt, g1, **kw)
        hm, q = norm_proj(xt, p["norm_ffn_g"][i], sc2, sh2, w["peer_q"][i], want_h="packed", **kw)
        delta = peer_mix(hm, q, w["peer_keys"][i], w["peer_u"][i], w["peer_v"][i], tm=min(256, t))
        gate = g2
    zeros = jnp.zeros_like(gate)
    xt, y = norm_proj(xt, p["final_norm_g"], zeros, zeros, delta=delta, gate=gate, want_h=True, seq_len=seq, tm=tm)
    return y.reshape(b, seq, D_MODEL), jnp.stack(gla_new), jnp.stack(rg_h_new), jnp.stack(rg_conv_new)


def kernel(x_prompt, x_sample, c_prompt, c_sample, state_gla, state_rglru_h, state_rglru_conv, ada_w, ada_b, norm_mix_g, norm_ffn_g, gla_w_in, gla_w_gate2, gla_b_gate, gla_onorm_g, gla_w_out, rg_w_in, rg_conv_w, rg_conv_b, rg_w_a, rg_b_a, rg_w_x, rg_b_x, rg_lambda, rg_w_out, peer_w_q, peer_sub_keys, peer_u, peer_v, final_norm_g):
    p = dict(norm_mix_g=norm_mix_g, norm_ffn_g=norm_ffn_g, gla_w_in=gla_w_in, gla_w_gate2=gla_w_gate2,
             gla_b_gate=gla_b_gate, gla_onorm_g=gla_onorm_g, gla_w_out=gla_w_out, rg_w_in=rg_w_in,
             rg_conv_w=rg_conv_w, rg_conv_b=rg_conv_b, rg_w_a=rg_w_a, rg_b_a=rg_b_a, rg_w_x=rg_w_x,
             rg_b_x=rg_b_x, rg_lambda=rg_lambda, rg_w_out=rg_w_out, peer_w_q=peer_w_q,
             peer_sub_keys=peer_sub_keys, peer_u=peer_u, peer_v=peer_v, final_norm_g=final_norm_g)
    w = _prep_weights(p)
    nb_p, nb_s = x_prompt.shape[0], x_sample.shape[0]
    n_gla, n_rg = state_gla.shape[0], state_rglru_h.shape[0]
    mod = adaln(jnp.concatenate([c_prompt, c_sample], axis=0), ada_w, ada_b)
    gla0 = jnp.zeros((n_gla, nb_p) + state_gla.shape[2:], f32)
    h0 = jnp.zeros((n_rg, nb_p, D_RNN), f32)
    conv0 = jnp.zeros((n_rg, nb_p, CONV_W - 1, D_RNN), f32)
    per = nb_p // PROMPT_STREAMS
    parts = []
    for s in range(PROMPT_STREAMS):
        rows = slice(s * per, (s + 1) * per)
        parts.append(_trunk(x_prompt[rows], mod[:, rows], 0, gla0[:, rows], h0[:, rows], conv0[:, rows], p, w,
                            decode=False))
    y_p = jnp.concatenate([o[0] for o in parts], axis=0)
    gla_p, h_p, conv_p = (jnp.concatenate([o[k] for o in parts], axis=1) for k in (1, 2, 3))
    y_s, gla_s, h_s, conv_s = _trunk(x_sample, mod[:, nb_p:], PAST_LEN, state_gla, state_rglru_h, state_rglru_conv,
                                     p, w, decode=True)
    return (y_p, y_s, gla_p, gla_s, h_p, h_s, conv_p, conv_s)
```

```python
import functools

import jax
import jax.numpy as jnp
from jax import lax
from jax.experimental import pallas as pl
from jax.experimental.pallas import tpu as pltpu
from jax.experimental.pallas import tpu_sc as plsc

f32 = jnp.float32
bf16 = jnp.bfloat16
HIGHEST = lax.Precision.HIGHEST

D_MODEL = 1024
DEPTH = 4
GLA_HEADS = 4
GLA_DK = 512
GLA_DV = 1024
GLA_HDK = 128
GLA_HDV = 256
GLA_RANK = 16
GLA_TAU = 16.0
GLA_CHUNK = 64
D_RNN = 1280
RG_BLOCKS = 8
RG_BLOCK = 160
CONV_W = 4
RG_C = 8.0
PEER_HEADS = 8
PEER_NKEYS = 128
PEER_DHALF = 128
PEER_TOPK = 16
PEER_ROWS = PEER_HEADS * PEER_TOPK
PAST_LEN = 16384
EPS = 1e-6

SC_CORES = 2
SC_SUBCORES = 16
SC_LANES = 16
SC_WORKERS = SC_CORES * SC_SUBCORES
LANES = 128
HALF_W = D_MODEL // 2
SC_CHUNK_ROWS = 32
SC_CHUNKS = PEER_ROWS // SC_CHUNK_ROWS
SC_RING = 4
SC_TOK_BLOCK = 32
PROMPT_STREAMS = 4


VMEM_CAP_BYTES = 60 * 2 ** 20


def _vmem_limit(block_bytes, temp_bytes):
    return min(VMEM_CAP_BYTES, 2 * block_bytes + temp_bytes)


def _rms(x, gain):
    return x * lax.rsqrt(jnp.mean(x * x, axis=-1, keepdims=True) + EPS) * gain


def _bf16_bits(v):
    return lax.bitcast_convert_type(v.astype(bf16).astype(f32), jnp.int32)


def _pack_halves(v):
    u = _bf16_bits(v)
    half = v.shape[-1] // 2
    return (u[:, half:] & jnp.int32(-65536)) | lax.shift_right_logical(u[:, :half], 16)


def _pack_twice(v):
    u = _bf16_bits(v)
    return u | lax.shift_right_logical(u, 16)


def _adaln_kernel(c_ref, w_ref, b_ref, o_ref):
    cond = jax.nn.silu(c_ref[...])
    o_ref[0] = jnp.dot(cond, w_ref[0], precision=HIGHEST, preferred_element_type=f32) + b_ref[0]


def adaln(c, ada_w, ada_b):
    bc = c.shape[0]
    tn = 1024
    return pl.pallas_call(
        _adaln_kernel,
        grid=(DEPTH, 6 * D_MODEL // tn),
        in_specs=[
            pl.BlockSpec((bc, D_MODEL), lambda i, j: (0, 0)),
            pl.BlockSpec((1, D_MODEL, tn), lambda i, j: (i, 0, j)),
            pl.BlockSpec((1, 1, tn), lambda i, j: (i, 0, j)),
        ],
        out_specs=pl.BlockSpec((1, bc, tn), lambda i, j: (i, 0, j)),
        out_shape=jax.ShapeDtypeStruct((DEPTH, bc, 6 * D_MODEL), f32),
        name="adaln",
    )(c, ada_w, ada_b.reshape(DEPTH, 1, 6 * D_MODEL))


def _proj_kernel(*refs, has_delta, has_w, want_h):
    it = iter(refs)
    x_ref = next(it)
    delta_ref = next(it) if has_delta else None
    gate_ref = next(it) if has_delta else None
    g_ref, sc_ref, sh_ref = next(it), next(it), next(it)
    w_ref = next(it) if has_w else None
    xo_ref = next(it) if has_delta else None
    h_ref = next(it) if want_h else None
    y_ref = next(it) if has_w else None
    hb_ref = next(it) if has_w else None

    def prologue():
        x = x_ref[...]
        if has_delta:
            x = x + gate_ref[...].reshape(-1, D_MODEL) * delta_ref[...]
            xo_ref[...] = x
        h = _rms(x, g_ref[...])
        h = h * (1.0 + sc_ref[...].reshape(-1, D_MODEL)) + sh_ref[...].reshape(-1, D_MODEL)
        if want_h == "packed":
            h_ref[...] = _pack_halves(h)
        elif want_h:
            h_ref[...] = h
        if has_w:
            hb_ref[...] = h.astype(bf16)

    if has_w:
        pl.when(pl.program_id(1) == 0)(prologue)
        y_ref[...] = jnp.dot(hb_ref[...], w_ref[...], preferred_element_type=f32)
    else:
        prologue()


def _mod_spec(mod, tm, seq_len):
    if mod.ndim == 3:
        per = seq_len // tm
        return pl.BlockSpec((1, 1, D_MODEL), lambda i, j: (i // per, 0, 0))
    return pl.BlockSpec((tm, D_MODEL), lambda i, j: (i, 0))


def norm_proj(x, gain, scale, shift, w=None, *, delta=None, gate=None, want_h=False, seq_len, tm, tn=None):
    t = x.shape[0]
    has_delta, has_w = delta is not None, w is not None
    n = w.shape[1] if has_w else 0
    tn = (tn or n) if has_w else 0
    grid = (t // tm, n // tn if has_w else 1)
    row_bytes = tm * D_MODEL * 4
    block_bytes = row_bytes * (1 + 2 * has_delta + bool(want_h)) + D_MODEL * tn * 2 + tm * tn * 4
    temp_bytes = 4 * row_bytes + tm * tn * 4 + tm * D_MODEL * 2
    row = pl.BlockSpec((tm, D_MODEL), lambda i, j: (i, 0))
    args, specs = [x], [row]
    if has_delta:
        args += [delta, gate]
        specs += [row, _mod_spec(gate, tm, seq_len)]
    args += [gain.reshape(1, D_MODEL), scale, shift]
    specs += [pl.BlockSpec((1, D_MODEL), lambda i, j: (0, 0)), _mod_spec(scale, tm, seq_len),
              _mod_spec(shift, tm, seq_len)]
    out_shapes, out_specs, scratch = [], [], []
    if has_delta:
        out_shapes.append(jax.ShapeDtypeStruct((t, D_MODEL), f32))
        out_specs.append(row)
    if want_h == "packed":
        out_shapes.append(jax.ShapeDtypeStruct((t, HALF_W), jnp.int32))
        out_specs.append(pl.BlockSpec((tm, HALF_W), lambda i, j: (i, 0)))
    elif want_h:
        out_shapes.append(jax.ShapeDtypeStruct((t, D_MODEL), f32))
        out_specs.append(row)
    if has_w:
        args.append(w)
        specs.append(pl.BlockSpec((D_MODEL, tn), lambda i, j: (0, j)))
        out_shapes.append(jax.ShapeDtypeStruct((t, n), f32))
        out_specs.append(pl.BlockSpec((tm, tn), lambda i, j: (i, j)))
        scratch.append(pltpu.VMEM((tm, D_MODEL), bf16))
    return pl.pallas_call(
        functools.partial(_proj_kernel, has_delta=has_delta, has_w=has_w, want_h=want_h),
        grid=grid, in_specs=specs, out_specs=out_specs, out_shape=out_shapes,
        scratch_shapes=scratch,
        compiler_params=pltpu.CompilerParams(dimension_semantics=("parallel", "arbitrary"),
                                             vmem_limit_bytes=_vmem_limit(block_bytes, temp_bytes)),
        name="norm_proj",
    )(*args)


def _out_proj_kernel(a_ref, w_ref, x_ref, gate_ref, o_ref):
    y = jnp.dot(a_ref[...], w_ref[...], preferred_element_type=f32)
    o_ref[...] = x_ref[...] + gate_ref[...].reshape(-1, D_MODEL) * y


def out_proj(a, w, x, gate, *, seq_len, tm):
    t, k = a.shape
    row = pl.BlockSpec((tm, D_MODEL), lambda i, j: (i, 0))
    return pl.pallas_call(
        _out_proj_kernel,
        grid=(t // tm, 1),
        in_specs=[pl.BlockSpec((tm, k), lambda i, j: (i, 0)),
                  pl.BlockSpec((k, D_MODEL), lambda i, j: (0, 0)),
                  row, _mod_spec(gate, tm, seq_len)],
        out_specs=row,
        out_shape=jax.ShapeDtypeStruct((t, D_MODEL), f32),
        name="out_proj",
    )(a, w, x, gate)


def _gla_kernel(q_ref, k_ref, v_ref, g_ref, lr_ref, wg_ref, bg_ref, on_ref, s0_ref,
                o_ref, so_ref, s_scr, *, bb, chunk, valid):
    c = pl.program_id(1)

    @pl.when(c == 0)
    def _():
        s_scr[...] = s0_ref[...]

    rows = lax.broadcasted_iota(jnp.int32, (chunk, chunk), 0)
    cols = lax.broadcasted_iota(jnp.int32, (chunk, chunk), 1)
    causal = rows >= cols
    tri = causal.astype(f32)
    ones = jnp.ones((chunk, GLA_HDV), f32)
    nt = (((1,), (1,)), ((), ()))
    tn = (((0,), (0,)), ((), ()))
    for bi in range(bb):
        gate_in = jnp.dot(lr_ref[bi].astype(bf16), wg_ref[...], preferred_element_type=f32) + bg_ref[...]
        log_a = jax.nn.log_sigmoid(gate_in) / GLA_TAU
        if valid < chunk:
            log_a = jnp.where(lax.broadcasted_iota(jnp.int32, log_a.shape, 0) < valid, log_a, 0.0)
        for h in range(GLA_HEADS):
            ks = slice(h * GLA_HDK, (h + 1) * GLA_HDK)
            vs = slice(h * GLA_HDV, (h + 1) * GLA_HDV)
            la = log_a[:, ks]
            b = jnp.dot(tri, la, precision=HIGHEST, preferred_element_type=f32)
            q = q_ref[bi, :, ks] * (GLA_HDK ** -0.5)
            k = k_ref[bi, :, ks]
            v = v_ref[bi, :, vs].astype(bf16)
            qd = (q * jnp.exp(b)).astype(bf16)
            kd = (k * jnp.exp(-b)).astype(bf16)
            scores = lax.dot_general(qd, kd, nt, preferred_element_type=f32)
            scores = jnp.where(causal, scores, 0.0).astype(bf16)
            s_old = s_scr[bi, h]
            o = (jnp.dot(scores, v, preferred_element_type=f32)
                 + jnp.dot(qd, s_old.astype(bf16), preferred_element_type=f32))
            b_end = b[chunk - 1:chunk, :]
            k_end = (k * jnp.exp(b_end - b)).astype(bf16)
            decay = jnp.exp(lax.dot_general(la, ones, tn, precision=HIGHEST, preferred_element_type=f32))
            s_scr[bi, h] = s_old * decay + lax.dot_general(k_end, v, tn, preferred_element_type=f32)
            on = _rms(o, on_ref[...])
            o_ref[bi, :, vs] = (on * jax.nn.silu(g_ref[bi, :, vs])).astype(o_ref.dtype)

    @pl.when(c == pl.num_programs(1) - 1)
    def _():
        so_ref[...] = s_scr[...]


def gla_mix(y, lr, w_gate2p, b_gate, onorm_g, s0, *, bb, chunk, valid):
    b, seq, _ = y.shape
    grid = (b // bb, seq // chunk)

    def col(width, blk):
        return pl.BlockSpec((bb, chunk, width), lambda i, c: (i, c, blk))

    st = pl.BlockSpec((bb, GLA_HEADS, GLA_HDK, GLA_HDV), lambda i, c: (i, 0, 0, 0))
    return pl.pallas_call(
        functools.partial(_gla_kernel, bb=bb, chunk=chunk, valid=valid),
        grid=grid,
        in_specs=[col(GLA_DK, 0), col(GLA_DK, 1), col(GLA_DV, 1), col(GLA_DV, 2),
                  pl.BlockSpec((bb, chunk, LANES), lambda i, c: (i, c, 0)),
                  pl.BlockSpec((LANES, GLA_DK), lambda i, c: (0, 0)),
                  pl.BlockSpec((1, GLA_DK), lambda i, c: (0, 0)),
                  pl.BlockSpec((1, GLA_HDV), lambda i, c: (0, 0)),
                  st],
        out_specs=[pl.BlockSpec((bb, chunk, GLA_DV), lambda i, c: (i, c, 0)), st],
        out_shape=[jax.ShapeDtypeStruct((b, seq, GLA_DV), bf16),
                   jax.ShapeDtypeStruct(s0.shape, f32)],
        scratch_shapes=[pltpu.VMEM((bb, GLA_HEADS, GLA_HDK, GLA_HDV), f32)],
        compiler_params=pltpu.CompilerParams(dimension_semantics=("parallel", "arbitrary")),
        name="gla_mix",
    )(y, y, y, y, lr, w_gate2p, b_gate.reshape(1, GLA_DK), onorm_g.reshape(1, GLA_HDV), s0)


def _rg_gates(xconv, wa_ref, ba_ref, wx_ref, bx_ref, lam_ref):
    xb = xconv.astype(bf16)
    gate_r = jax.nn.sigmoid(jnp.dot(xb, wa_ref[...], preferred_element_type=f32) + ba_ref[...])
    gate_i = jax.nn.sigmoid(jnp.dot(xb, wx_ref[...], preferred_element_type=f32) + bx_ref[...])
    log_a = -RG_C * gate_r * jax.nn.softplus(-lam_ref[...])
    a = jnp.exp(log_a)
    mult = jnp.sqrt(-jnp.tanh(log_a) * (a * a + 1.0))
    return a, mult, gate_i


def _rg_prefill_kernel(y_ref, xb_ref, h0_ref, c0_ref, cw_ref, cb_ref, wa_ref, ba_ref, wx_ref, bx_ref,
                       lam_ref, o_ref, ho_ref, co_ref, buf, h_scr, *, tl, start):
    t = pl.program_id(1)
    top = 8

    @pl.when(t == 0)
    def _():
        h_scr[...] = h0_ref[0]
        buf[top - (CONV_W - 1):top, :] = c0_ref[0]

    x = xb_ref[0]
    buf[top:top + tl, :] = x
    xconv = cb_ref[...] + sum(buf[top - (CONV_W - 1) + j:top - (CONV_W - 1) + j + tl, :] * cw_ref[j:j + 1, :]
                              for j in range(CONV_W))
    a, mult, gate_i = _rg_gates(xconv, wa_ref, ba_ref, wx_ref, bx_ref, lam_ref)
    row = lax.broadcasted_iota(jnp.int32, (tl, D_RNN), 0)
    reset = (row + t * tl + start) == 0
    a = jnp.where(reset, 0.0, a)
    mult = jnp.where(reset, 1.0, mult)
    b = mult * gate_i * xconv
    s = 1
    while s < tl:
        keep = row >= s
        b = jnp.where(keep, a * pltpu.roll(b, s, 0) + b, b)
        a = jnp.where(keep, a * pltpu.roll(a, s, 0), a)
        s *= 2
    hs = a * h_scr[...] + b
    o_ref[0] = (hs * jax.nn.gelu(y_ref[0])).astype(o_ref.dtype)
    h_scr[...] = hs[tl - 1:tl, :]
    buf[top - (CONV_W - 1):top, :] = x[tl - (CONV_W - 1):tl, :]

    @pl.when(t == pl.num_programs(1) - 1)
    def _():
        ho_ref[0] = hs[tl - 1:tl, :]
        co_ref[0] = x[tl - (CONV_W - 1):tl, :]


def _rg_vec(v):
    return v.reshape(1, D_RNN)


def rglru_prefill(y, h0, conv0, conv_w, conv_b, wa, ba, wx, bx, lam, *, start, tl):
    b, seq, _ = y.shape
    assert seq % tl == 0 and tl >= CONV_W - 1
    vec = pl.BlockSpec((1, D_RNN), lambda i, t: (0, 0))
    mat = pl.BlockSpec((D_RNN, D_RNN), lambda i, t: (0, 0))
    return pl.pallas_call(
        functools.partial(_rg_prefill_kernel, tl=tl, start=start),
        grid=(b, seq // tl),
        in_specs=[pl.BlockSpec((1, tl, D_RNN), lambda i, t: (i, t, 0)),
                  pl.BlockSpec((1, tl, D_RNN), lambda i, t: (i, t, 1)),
                  pl.BlockSpec((1, 1, D_RNN), lambda i, t: (i, 0, 0)),
                  pl.BlockSpec((1, CONV_W - 1, D_RNN), lambda i, t: (i, 0, 0)),
                  pl.BlockSpec((CONV_W, D_RNN), lambda i, t: (0, 0)),
                  vec, mat, vec, mat, vec, vec],
        out_specs=[pl.BlockSpec((1, tl, D_RNN), lambda i, t: (i, t, 0)),
                   pl.BlockSpec((1, 1, D_RNN), lambda i, t: (i, 0, 0)),
                   pl.BlockSpec((1, CONV_W - 1, D_RNN), lambda i, t: (i, 0, 0))],
        out_shape=[jax.ShapeDtypeStruct((b, seq, D_RNN), bf16),
                   jax.ShapeDtypeStruct((b, 1, D_RNN), f32),
                   jax.ShapeDtypeStruct((b, CONV_W - 1, D_RNN), f32)],
        scratch_shapes=[pltpu.VMEM((8 + tl, D_RNN), f32), pltpu.VMEM((1, D_RNN), f32)],
        compiler_params=pltpu.CompilerParams(dimension_semantics=("parallel", "arbitrary")),
        name="rglru_prefill",
    )(y, y, h0.reshape(b, 1, D_RNN), conv0, conv_w, _rg_vec(conv_b), wa, _rg_vec(ba), wx, _rg_vec(bx),
      _rg_vec(lam))


def _rg_decode_kernel(y_ref, xb_ref, h0_ref, c0_ref, cw_ref, cb_ref, wa_ref, ba_ref, wx_ref, bx_ref,
                      lam_ref, o_ref, ho_ref, *, steps, start):
    xc = [c0_ref[j] for j in range(CONV_W - 1)] + [xb_ref[s] for s in range(steps)]
    h = h0_ref[...]
    for s in range(steps):
        xconv = cb_ref[...] + sum(xc[s + j] * cw_ref[j:j + 1, :] for j in range(CONV_W))
        a, mult, gate_i = _rg_gates(xconv, wa_ref, ba_ref, wx_ref, bx_ref, lam_ref)
        if start + s == 0:
            a, mult = 0.0, 1.0
        h = a * h + mult * gate_i * xconv
        o_ref[s] = (h * jax.nn.gelu(y_ref[s])).astype(o_ref.dtype)
    ho_ref[...] = h


def rglru_decode(y, h0, conv0, conv_w, conv_b, wa, ba, wx, bx, lam, *, start):
    steps, b, _ = y.shape
    whole = lambda *shape: pl.BlockSpec(shape, lambda i: (0,) * len(shape))
    vec, mat = whole(1, D_RNN), whole(D_RNN, D_RNN)
    return pl.pallas_call(
        functools.partial(_rg_decode_kernel, steps=steps, start=start),
        grid=(1,),
        in_specs=[pl.BlockSpec((steps, b, D_RNN), lambda i: (0, 0, 0)),
                  pl.BlockSpec((steps, b, D_RNN), lambda i: (0, 0, 1)),
                  whole(b, D_RNN), whole(CONV_W - 1, b, D_RNN), whole(CONV_W, D_RNN),
                  vec, mat, vec, mat, vec, vec],
        out_specs=[whole(steps, b, D_RNN), whole(b, D_RNN)],
        out_shape=[jax.ShapeDtypeStruct((steps, b, D_RNN), bf16),
                   jax.ShapeDtypeStruct((b, D_RNN), f32)],
        name="rglru_decode",
    )(y, y, h0, conv0, conv_w, _rg_vec(conv_b), wa, _rg_vec(ba), wx, _rg_vec(bx), _rg_vec(lam))


def _top_rows(v, payload=None):
    nrows = v.shape[0]
    rowi = lax.broadcasted_iota(jnp.int32, v.shape, 0)
    vals, picks = [], []
    for _ in range(PEER_TOPK):
        m = jnp.max(v, axis=0, keepdims=True)
        i = jnp.min(jnp.where(v == m, rowi, nrows), axis=0, keepdims=True)
        hit = rowi == i
        vals.append(m)
        picks.append(i if payload is None else jnp.max(jnp.where(hit, payload, -1), axis=0, keepdims=True))
        v = jnp.where(hit, -jnp.inf, v)
    return jnp.concatenate(vals, axis=0), jnp.concatenate(picks, axis=0)


def _pair_rows(a0, a1, fill):
    sub = lax.broadcasted_iota(jnp.int32, (8,) + a0.shape[1:], 0)
    blocks = [a0[0:1] + a1]
    for a in range(1, 8):
        blk = a0[a:a + 1] + a1[0:8]
        nb = PEER_TOPK // (a + 1)
        if fill is not None and nb < 8:
            blk = jnp.where(sub < nb, blk, fill)
        blocks.append(blk)
    blocks.append(a0[8:16] + a1[0:1])
    return jnp.concatenate(blocks, axis=0)


def _peer_topk_kernel(q_ref, keys_ref, idx_ref, g_ref, *, tm):
    nt = (((1,), (1,)), ((), ()))
    idx_rows, g_rows = [], []
    for h in range(PEER_HEADS):
        top = []
        for p in range(2):
            col = (h * 2 + p) * PEER_DHALF
            qs = q_ref[:, col:col + PEER_DHALF].astype(bf16)
            s = lax.dot_general(keys_ref[h, p], qs, nt, preferred_element_type=f32)
            top.append(_top_rows(s))
        (s0, i0), (s1, i1) = top
        best, idx = _top_rows(_pair_rows(s0, s1, -jnp.inf), _pair_rows(i0 * PEER_NKEYS, i1, None))
        e = jnp.exp(best - best[0:1])
        g_rows.append(e / jnp.sum(e, axis=0, keepdims=True))
        idx_rows.append(idx.astype(f32))
    idx_ref[...] = jnp.concatenate(idx_rows, axis=0).T.astype(jnp.int32)
    g_ref[...] = jnp.concatenate(g_rows, axis=0).T


def peer_topk(q, keys, *, tm):
    t = q.shape[0]
    out = pl.BlockSpec((tm, PEER_ROWS), lambda i: (i, 0))
    return pl.pallas_call(
        functools.partial(_peer_topk_kernel, tm=tm),
        grid=(t // tm,),
        in_specs=[pl.BlockSpec((tm, 2 * PEER_HEADS * PEER_DHALF), lambda i: (i, 0)),
                  pl.BlockSpec((PEER_HEADS, 2, PEER_NKEYS, PEER_DHALF), lambda i: (0, 0, 0, 0))],
        out_specs=[out, out],
        out_shape=[jax.ShapeDtypeStruct((t, PEER_ROWS), jnp.int32),
                   jax.ShapeDtypeStruct((t, PEER_ROWS), f32)],
        name="peer_topk",
    )(q, keys)


def _peer_mid_kernel(dots_ref, g_ref, o_ref):
    o_ref[...] = _pack_twice(g_ref[...] * jax.nn.gelu(dots_ref[...]))


def peer_mid(dots, g, *, tm):
    t = dots.shape[0]
    blk = pl.BlockSpec((tm, PEER_ROWS), lambda i: (i, 0))
    return pl.pallas_call(
        _peer_mid_kernel,
        grid=(t // tm,),
        in_specs=[blk, blk],
        out_specs=blk,
        out_shape=jax.ShapeDtypeStruct((t, PEER_ROWS), jnp.int32),
        name="peer_mid",
    )(dots, g)


def _as_pairs(w):
    return plsc.bitcast(w, bf16)


def _sum4_split(p):
    s = plsc.bitcast((p[0] + p[1]) + (p[2] + p[3]), jnp.int32)
    return plsc.bitcast(s << 16, f32), plsc.bitcast(s & jnp.int32(-65536), f32)


def _sc_worker_base(tokens_per_worker):
    wid = lax.axis_index("s") * SC_CORES + lax.axis_index("c")
    return wid * tokens_per_worker


def _sc_kernel(body, t, out_width, scratch):
    mesh = plsc.VectorSubcoreMesh(core_axis_name="c", subcore_axis_name="s")
    return pl.kernel(
        body, mesh=mesh,
        out_type=jax.ShapeDtypeStruct((t, out_width), f32),
        scratch_types=scratch,
        compiler_params=pltpu.CompilerParams(needs_layout_passes=False),
    )


def _sc_tok_block(tokens_per_worker):
    tb = min(SC_TOK_BLOCK, tokens_per_worker)
    assert tokens_per_worker % tb == 0 and (tb * SC_CHUNKS) % SC_RING == 0
    return tb


def _sc_gather_loop(tab_hbm, idx_v, rows_v, sems, compute, tb):
    nchunks = tb * SC_CHUNKS

    def gather(c, slot):
        src = tab_hbm.at[idx_v.at[c // SC_CHUNKS, c % SC_CHUNKS]]
        return pltpu.make_async_copy(src, rows_v.at[slot], sems.at[slot])

    for slot in range(SC_RING - 1):
        gather(slot, slot).start()

    def ring_turn(turn, carry):
        for slot in range(SC_RING):
            c = turn * SC_RING + slot
            ahead = c + SC_RING - 1

            @pl.when(ahead < nchunks)
            def _():
                gather(ahead, (slot + SC_RING - 1) % SC_RING).start()

            gather(c, slot).wait()
            compute(c // SC_CHUNKS, c % SC_CHUNKS, rows_v.at[slot])
        return carry

    lax.fori_loop(0, nchunks // SC_RING, ring_turn, 0)


def sc_peer_dots(x, idx, tab):
    t = x.shape[0]
    tpw = t // SC_WORKERS
    tb = _sc_tok_block(tpw)
    rpg = 8
    quads = HALF_W // SC_LANES // 4

    def body(x_hbm, idx_hbm, tab_hbm, out_hbm, x_v, idx_v, rows_v, part_v, out_v, sems):
        base = _sc_worker_base(tpw)
        lane = lax.iota(jnp.int32, SC_LANES)

        def block(blk, carry):
            t0 = base + blk * tb
            pltpu.sync_copy(x_hbm.at[pl.ds(t0, tb)], x_v)
            pltpu.sync_copy(idx_hbm.at[pl.ds(t0, tb)], idx_v)

            def compute(tok, part, rows):
                def group(g, c2):
                    def words(j4, accs):
                        cols = [pl.ds((j4 * 4 + q) * SC_LANES, SC_LANES) for q in range(4)]
                        xs = [_as_pairs(x_v[tok, c]) for c in cols]
                        new = []
                        for r in range(rpg):
                            lo, hi = _sum4_split([_as_pairs(rows[g * rpg + r, c]) * xq for c, xq in zip(cols, xs)])
                            new.append(accs[r] + (lo + hi))
                        return tuple(new)

                    accs = lax.fori_loop(0, quads, words, tuple(jnp.zeros((SC_LANES,), f32) for _ in range(rpg)))
                    for r in range(rpg):
                        part_v[pl.ds((g * rpg + r) * SC_LANES, SC_LANES)] = accs[r]
                    return c2

                lax.fori_loop(0, SC_CHUNK_ROWS // rpg, group, 0)

                def fold(rb, c3):
                    first = rb * (SC_LANES * SC_LANES) + lane * SC_LANES
                    s = plsc.load_gather(part_v, [first])
                    for c in range(1, SC_LANES):
                        s = s + plsc.load_gather(part_v, [first + c])
                    out_v[tok, pl.ds(part * SC_CHUNK_ROWS + rb * SC_LANES, SC_LANES)] = s
                    return c3

                lax.fori_loop(0, SC_CHUNK_ROWS // SC_LANES, fold, 0)

            _sc_gather_loop(tab_hbm, idx_v, rows_v, sems, compute, tb)
            pltpu.sync_copy(out_v, out_hbm.at[pl.ds(t0, tb)])
            return carry

        lax.fori_loop(0, tpw // tb, block, 0)

    scratch = [pltpu.VMEM((tb, HALF_W), jnp.int32),
               pltpu.VMEM((tb, SC_CHUNKS, SC_CHUNK_ROWS), jnp.int32),
               pltpu.VMEM((SC_RING, SC_CHUNK_ROWS, HALF_W), jnp.int32),
               pltpu.VMEM((SC_CHUNK_ROWS * SC_LANES,), f32),
               pltpu.VMEM((tb, PEER_ROWS), f32),
               pltpu.SemaphoreType.DMA((SC_RING,))]
    return _sc_kernel(body, t, PEER_ROWS, scratch)(x, idx, tab)


def sc_peer_mix(wgt, idx, tab):
    t = wgt.shape[0]
    tpw = t // SC_WORKERS
    tb = _sc_tok_block(tpw)
    wpg = 8
    ngroups = HALF_W // SC_LANES // wpg

    def body(w_hbm, idx_hbm, tab_hbm, out_hbm, w_v, idx_v, rows_v, out_v, sems):
        base = _sc_worker_base(tpw)
        zero = jnp.zeros((SC_LANES,), jnp.int32)
        zero_f = jnp.zeros((SC_LANES,), f32)

        def block(blk, carry):
            t0 = base + blk * tb
            pltpu.sync_copy(w_hbm.at[pl.ds(t0, tb)], w_v)
            pltpu.sync_copy(idx_hbm.at[pl.ds(t0, tb)], idx_v)

            def compute(tok, part, rows):
                @pl.when(part == 0)
                def _():
                    def clear(i, c4):
                        out_v[tok, pl.ds(i * SC_LANES, SC_LANES)] = zero_f
                        return c4

                    lax.fori_loop(0, D_MODEL // SC_LANES, clear, 0)

                def group(g, c2):
                    def out_at(jj, hw):
                        return out_v.at[tok, pl.ds(hw * HALF_W + (g * wpg + jj) * SC_LANES, SC_LANES)]

                    def quad(k4, accs):
                        wk = [_as_pairs(plsc.load_gather(w_v.at[tok], [zero + (part * SC_CHUNK_ROWS + k4 * 4 + q)]))
                              for q in range(4)]
                        new = []
                        for jj in range(wpg):
                            col = pl.ds((g * wpg + jj) * SC_LANES, SC_LANES)
                            lo, hi = _sum4_split([_as_pairs(rows[k4 * 4 + q, col]) * wk[q] for q in range(4)])
                            new.append(accs[2 * jj] + lo)
                            new.append(accs[2 * jj + 1] + hi)
                        return tuple(new)

                    init = tuple(out_at(jj, hw)[...] for jj in range(wpg) for hw in range(2))
                    accs = lax.fori_loop(0, SC_CHUNK_ROWS // 4, quad, init)
                    for jj in range(wpg):
                        for hw in range(2):
                            out_at(jj, hw)[...] = accs[2 * jj + hw]
                    return c2

                lax.fori_loop(0, ngroups, group, 0)

            _sc_gather_loop(tab_hbm, idx_v, rows_v, sems, compute, tb)
            pltpu.sync_copy(out_v, out_hbm.at[pl.ds(t0, tb)])
            return carry

        lax.fori_loop(0, tpw // tb, block, 0)

    scratch = [pltpu.VMEM((tb, PEER_ROWS), jnp.int32),
               pltpu.VMEM((tb, SC_CHUNKS, SC_CHUNK_ROWS), jnp.int32),
               pltpu.VMEM((SC_RING, SC_CHUNK_ROWS, HALF_W), jnp.int32),
               pltpu.VMEM((tb, D_MODEL), f32),
               pltpu.SemaphoreType.DMA((SC_RING,))]
    return _sc_kernel(body, t, D_MODEL, scratch)(wgt, idx, tab)


def pack_table(tab):
    b = lax.bitcast_convert_type(tab.astype(bf16), jnp.uint16).astype(jnp.uint32)
    return lax.bitcast_convert_type(b[:, :HALF_W] | (b[:, HALF_W:] << 16), jnp.int32)


def peer_mix(hm, q, keys, u_packed, v_packed, *, tm):
    t = hm.shape[0]
    idx, g = peer_topk(q, keys, tm=tm)
    idx = idx.reshape(t, SC_CHUNKS, SC_CHUNK_ROWS)
    dots = sc_peer_dots(hm, idx, u_packed)
    wgt = peer_mid(dots, g, tm=tm)
    return sc_peer_mix(wgt, idx, v_packed)


def _block_diag(w):
    nb, n, _ = w.shape
    eye = jnp.eye(nb, dtype=w.dtype)
    return (eye[:, None, :, None] * w[:, :, None, :]).reshape(nb * n, nb * n)


def _prep_weights(p):
    w = {}
    w["gla_main"] = p["gla_w_in"][:, :, :2 * GLA_DK + 2 * GLA_DV].astype(bf16)
    w["gla_lr"] = jnp.pad(p["gla_w_in"][:, :, 2 * GLA_DK + 2 * GLA_DV:],
                          ((0, 0), (0, 0), (0, LANES - GLA_RANK))).astype(bf16)
    w["gla_gate2"] = jnp.pad(p["gla_w_gate2"], ((0, 0), (0, LANES - GLA_RANK), (0, 0))).astype(bf16)
    w["gla_out"] = p["gla_w_out"].astype(bf16)
    w["rg_in"] = p["rg_w_in"].astype(bf16)
    w["rg_wa"] = jax.vmap(_block_diag)(p["rg_w_a"]).astype(bf16)
    w["rg_wx"] = jax.vmap(_block_diag)(p["rg_w_x"]).astype(bf16)
    w["rg_out"] = p["rg_w_out"].astype(bf16)
    w["peer_q"] = p["peer_w_q"].astype(bf16)
    w["peer_keys"] = p["peer_sub_keys"].astype(bf16)
    w["peer_u"] = jax.vmap(pack_table)(p["peer_u"])
    w["peer_v"] = jax.vmap(pack_table)(p["peer_v"])
    return w


def _trunk(x, mod, start, gla_s0, rg_h0, rg_conv0, p, w, *, decode):
    b, seq, _ = x.shape
    t = b * seq
    tm = min(512, t if decode else seq)
    xt = x.reshape(t, D_MODEL)
    if decode:
        chunk = 16
        bb = 4
    else:
        chunk = min(GLA_CHUNK, seq)
        bb = 1
    delta = gate = None
    gla_new, rg_h_new, rg_conv_new = [], [], []
    for i in range(DEPTH):
        m = mod[i].reshape(b, 6, D_MODEL)
        if decode:
            mods = [jnp.repeat(m[:, j], seq, axis=0) for j in range(6)]
        else:
            mods = [m[:, j:j + 1] for j in range(6)]
        sh1, sc1, g1, sh2, sc2, g2 = mods
        kw = dict(seq_len=seq, tm=tm)
        j = i // 2
        if i % 2 == 0:
            if delta is None:
                (y,) = norm_proj(xt, p["norm_mix_g"][i], sc1, sh1, w["gla_main"][j], **kw)
            else:
                xt, y = norm_proj(xt, p["norm_mix_g"][i], sc1, sh1, w["gla_main"][j], delta=delta, gate=gate, **kw)
            (lr,) = norm_proj(xt, p["norm_mix_g"][i], sc1, sh1, w["gla_lr"][j], **kw)
            y3, lr3 = y.reshape(b, seq, -1), lr.reshape(b, seq, LANES)
            if decode:
                padding = ((0, 0), (0, chunk - seq), (0, 0))
                y3, lr3 = jnp.pad(y3, padding), jnp.pad(lr3, padding)
            o, s_new = gla_mix(y3, lr3, w["gla_gate2"][j], p["gla_b_gate"][j], p["gla_onorm_g"][j],
                               gla_s0[j], bb=bb, chunk=chunk, valid=min(seq, chunk))
            gla_new.append(s_new)
            o = o[:, :seq].reshape(t, GLA_DV)
            xt = out_proj(o, w["gla_out"][j], xt, g1, **kw)
        else:
            xt, y = norm_proj(xt, p["norm_mix_g"][i], sc1, sh1, w["rg_in"][j], delta=delta, gate=gate, **kw)
            rg_args = (p["rg_conv_w"][j], p["rg_conv_b"][j], w["rg_wa"][j], p["rg_b_a"][j], w["rg_wx"][j],
                       p["rg_b_x"][j], p["rg_lambda"][j])
            if decode:
                y3 = y.reshape(b, seq, 2 * D_RNN).transpose(1, 0, 2)
                o, h_new = rglru_decode(y3, rg_h0[j], rg_conv0[j].transpose(1, 0, 2), *rg_args, start=start)
                xbr = y.reshape(b, seq, 2 * D_RNN)[:, :, D_RNN:]
                conv_new = jnp.concatenate([rg_conv0[j], xbr], axis=1)[:, seq:]
                o = o.transpose(1, 0, 2).reshape(t, D_RNN)
            else:
                o, h_new, conv_new = rglru_prefill(y.reshape(b, seq, 2 * D_RNN), rg_h0[j], rg_conv0[j], *rg_args,
                                                   start=start, tl=min(256, seq))
                o, h_new = o.reshape(t, D_RNN), h_new.reshape(b, D_RNN)
            rg_h_new.append(h_new)
            rg_conv_new.append(conv_new)
            xt = out_proj(o, w["rg_out"][j], xt, g1, **kw)
        hm, q = norm_proj(xt, p["norm_ffn_g"][i], sc2, sh2, w["peer_q"][i], want_h="packed", **kw)
        delta = peer_mix(hm, q, w["peer_keys"][i], w["peer_u"][i], w["peer_v"][i], tm=min(256, t))
        gate = g2
    zeros = jnp.zeros_like(gate)
    xt, y = norm_proj(xt, p["final_norm_g"], zeros, zeros, delta=delta, gate=gate, want_h=True, seq_len=seq, tm=tm)
    return y.reshape(b, seq, D_MODEL), jnp.stack(gla_new), jnp.stack(rg_h_new), jnp.stack(rg_conv_new)


def kernel(x_prompt, x_sample, c_prompt, c_sample, state_gla, state_rglru_h, state_rglru_conv, ada_w, ada_b, norm_mix_g, norm_ffn_g, gla_w_in, gla_w_gate2, gla_b_gate, gla_onorm_g, gla_w_out, rg_w_in, rg_conv_w, rg_conv_b, rg_w_a, rg_b_a, rg_w_x, rg_b_x, rg_lambda, rg_w_out, peer_w_q, peer_sub_keys, peer_u, peer_v, final_norm_g):
    p = dict(norm_mix_g=norm_mix_g, norm_ffn_g=norm_ffn_g, gla_w_in=gla_w_in, gla_w_gate2=gla_w_gate2,
             gla_b_gate=gla_b_gate, gla_onorm_g=gla_onorm_g, gla_w_out=gla_w_out, rg_w_in=rg_w_in,
             rg_conv_w=rg_conv_w, rg_conv_b=rg_conv_b, rg_w_a=rg_w_a, rg_b_a=rg_b_a, rg_w_x=rg_w_x,
             rg_b_x=rg_b_x, rg_lambda=rg_lambda, rg_w_out=rg_w_out, peer_w_q=peer_w_q,
             peer_sub_keys=peer_sub_keys, peer_u=peer_u, peer_v=peer_v, final_norm_g=final_norm_g)
    w = _prep_weights(p)
    nb_p, nb_s = x_prompt.shape[0], x_sample.shape[0]
    n_gla, n_rg = state_gla.shape[0], state_rglru_h.shape[0]
    mod = adaln(jnp.concatenate([c_prompt, c_sample], axis=0), ada_w, ada_b)
    gla0 = jnp.zeros((n_gla, nb_p) + state_gla.shape[2:], f32)
    h0 = jnp.zeros((n_rg, nb_p, D_RNN), f32)
    conv0 = jnp.zeros((n_rg, nb_p, CONV_W - 1, D_RNN), f32)
    per = nb_p // PROMPT_STREAMS
    parts = []
    for s in range(PROMPT_STREAMS):
        rows = slice(s * per, (s + 1) * per)
        parts.append(_trunk(x_prompt[rows], mod[:, rows], 0, gla0[:, rows], h0[:, rows], conv0[:, rows], p, w,
                            decode=False))
    y_p = jnp.concatenate([o[0] for o in parts], axis=0)
    gla_p, h_p, conv_p = (jnp.concatenate([o[k] for o in parts], axis=1) for k in (1, 2, 3))
    y_s, gla_s, h_s, conv_s = _trunk(x_sample, mod[:, nb_p:], PAST_LEN, state_gla, state_rglru_h, state_rglru_conv,
                                     p, w, decode=True)
    return (y_p, y_s, gla_p, gla_s, h_p, h_s, conv_p, conv_s)
```

```python
import functools

import jax
import jax.numpy as jnp
from jax import lax
from jax.experimental import pallas as pl
from jax.experimental.pallas import tpu as pltpu
from jax.experimental.pallas import tpu_sc as plsc

f32 = jnp.float32
bf16 = jnp.bfloat16
HIGHEST = lax.Precision.HIGHEST

D_MODEL = 1024
DEPTH = 4
GLA_HEADS = 4
GLA_DK = 512
GLA_DV = 1024
GLA_HDK = 128
GLA_HDV = 256
GLA_RANK = 16
GLA_TAU = 16.0
GLA_CHUNK = 64
D_RNN = 1280
RG_BLOCKS = 8
RG_BLOCK = 160
CONV_W = 4
RG_C = 8.0
PEER_HEADS = 8
PEER_NKEYS = 128
PEER_DHALF = 128
PEER_TOPK = 16
PEER_ROWS = PEER_HEADS * PEER_TOPK
PAST_LEN = 16384
EPS = 1e-6

SC_CORES = 2
SC_SUBCORES = 16
SC_LANES = 16
SC_WORKERS = SC_CORES * SC_SUBCORES
LANES = 128
HALF_W = D_MODEL // 2
SC_CHUNK_ROWS = 32
SC_CHUNKS = PEER_ROWS // SC_CHUNK_ROWS
SC_RING = 4
SC_TOK_BLOCK = 16
PROMPT_STREAMS = 4


VMEM_CAP_BYTES = 60 * 2 ** 20


def _vmem_limit(block_bytes, temp_bytes):
    return min(VMEM_CAP_BYTES, 2 * block_bytes + temp_bytes)


def _rms(x, gain):
    return x * lax.rsqrt(jnp.mean(x * x, axis=-1, keepdims=True) + EPS) * gain


def _bf16_bits(v):
    return lax.bitcast_convert_type(v.astype(bf16).astype(f32), jnp.int32)


def _pack_halves(v):
    u = _bf16_bits(v)
    half = v.shape[-1] // 2
    return (u[:, half:] & jnp.int32(-65536)) | lax.shift_right_logical(u[:, :half], 16)


def _pack_twice(v):
    u = _bf16_bits(v)
    return u | lax.shift_right_logical(u, 16)


def _adaln_kernel(c_ref, w_ref, b_ref, o_ref):
    cond = jax.nn.silu(c_ref[...])
    o_ref[0] = jnp.dot(cond, w_ref[0], precision=HIGHEST, preferred_element_type=f32) + b_ref[0]


def adaln(c, ada_w, ada_b):
    bc = c.shape[0]
    tn = 1024
    return pl.pallas_call(
        _adaln_kernel,
        grid=(DEPTH, 6 * D_MODEL // tn),
        in_specs=[
            pl.BlockSpec((bc, D_MODEL), lambda i, j: (0, 0)),
            pl.BlockSpec((1, D_MODEL, tn), lambda i, j: (i, 0, j)),
            pl.BlockSpec((1, 1, tn), lambda i, j: (i, 0, j)),
        ],
        out_specs=pl.BlockSpec((1, bc, tn), lambda i, j: (i, 0, j)),
        out_shape=jax.ShapeDtypeStruct((DEPTH, bc, 6 * D_MODEL), f32),
        name="adaln",
    )(c, ada_w, ada_b.reshape(DEPTH, 1, 6 * D_MODEL))


def _proj_kernel(*refs, has_delta, has_w, want_h):
    it = iter(refs)
    x_ref = next(it)
    delta_ref = next(it) if has_delta else None
    gate_ref = next(it) if has_delta else None
    g_ref, sc_ref, sh_ref = next(it), next(it), next(it)
    w_ref = next(it) if has_w else None
    xo_ref = next(it) if has_delta else None
    h_ref = next(it) if want_h else None
    y_ref = next(it) if has_w else None
    hb_ref = next(it) if has_w else None

    def prologue():
        x = x_ref[...]
        if has_delta:
            x = x + gate_ref[...].reshape(-1, D_MODEL) * delta_ref[...]
            xo_ref[...] = x
        h = _rms(x, g_ref[...])
        h = h * (1.0 + sc_ref[...].reshape(-1, D_MODEL)) + sh_ref[...].reshape(-1, D_MODEL)
        if want_h == "packed":
            h_ref[...] = _pack_halves(h)
        elif want_h:
            h_ref[...] = h
        if has_w:
            hb_ref[...] = h.astype(bf16)

    if has_w:
        pl.when(pl.program_id(1) == 0)(prologue)
        y_ref[...] = jnp.dot(hb_ref[...], w_ref[...], preferred_element_type=f32)
    else:
        prologue()


def _mod_spec(mod, tm, seq_len):
    if mod.ndim == 3:
        per = seq_len // tm
        return pl.BlockSpec((1, 1, D_MODEL), lambda i, j: (i // per, 0, 0))
    return pl.BlockSpec((tm, D_MODEL), lambda i, j: (i, 0))


def norm_proj(x, gain, scale, shift, w=None, *, delta=None, gate=None, want_h=False, seq_len, tm, tn=None):
    t = x.shape[0]
    has_delta, has_w = delta is not None, w is not None
    n = w.shape[1] if has_w else 0
    tn = (tn or n) if has_w else 0
    grid = (t // tm, n // tn if has_w else 1)
    row_bytes = tm * D_MODEL * 4
    block_bytes = row_bytes * (1 + 2 * has_delta + bool(want_h)) + D_MODEL * tn * 2 + tm * tn * 4
    temp_bytes = 4 * row_bytes + tm * tn * 4 + tm * D_MODEL * 2
    row = pl.BlockSpec((tm, D_MODEL), lambda i, j: (i, 0))
    args, specs = [x], [row]
    if has_delta:
        args += [delta, gate]
        specs += [row, _mod_spec(gate, tm, seq_len)]
    args += [gain.reshape(1, D_MODEL), scale, shift]
    specs += [pl.BlockSpec((1, D_MODEL), lambda i, j: (0, 0)), _mod_spec(scale, tm, seq_len),
              _mod_spec(shift, tm, seq_len)]
    out_shapes, out_specs, scratch = [], [], []
    if has_delta:
        out_shapes.append(jax.ShapeDtypeStruct((t, D_MODEL), f32))
        out_specs.append(row)
    if want_h == "packed":
        out_shapes.append(jax.ShapeDtypeStruct((t, HALF_W), jnp.int32))
        out_specs.append(pl.BlockSpec((tm, HALF_W), lambda i, j: (i, 0)))
    elif want_h:
        out_shapes.append(jax.ShapeDtypeStruct((t, D_MODEL), f32))
        out_specs.append(row)
    if has_w:
        args.append(w)
        specs.append(pl.BlockSpec((D_MODEL, tn), lambda i, j: (0, j)))
        out_shapes.append(jax.ShapeDtypeStruct((t, n), f32))
        out_specs.append(pl.BlockSpec((tm, tn), lambda i, j: (i, j)))
        scratch.append(pltpu.VMEM((tm, D_MODEL), bf16))
    return pl.pallas_call(
        functools.partial(_proj_kernel, has_delta=has_delta, has_w=has_w, want_h=want_h),
        grid=grid, in_specs=specs, out_specs=out_specs, out_shape=out_shapes,
        scratch_shapes=scratch,
        compiler_params=pltpu.CompilerParams(dimension_semantics=("parallel", "arbitrary"),
                                             vmem_limit_bytes=_vmem_limit(block_bytes, temp_bytes)),
        name="norm_proj",
    )(*args)


def _out_proj_kernel(a_ref, w_ref, x_ref, gate_ref, o_ref):
    y = jnp.dot(a_ref[...], w_ref[...], preferred_element_type=f32)
    o_ref[...] = x_ref[...] + gate_ref[...].reshape(-1, D_MODEL) * y


def out_proj(a, w, x, gate, *, seq_len, tm):
    t, k = a.shape
    row = pl.BlockSpec((tm, D_MODEL), lambda i, j: (i, 0))
    return pl.pallas_call(
        _out_proj_kernel,
        grid=(t // tm, 1),
        in_specs=[pl.BlockSpec((tm, k), lambda i, j: (i, 0)),
                  pl.BlockSpec((k, D_MODEL), lambda i, j: (0, 0)),
                  row, _mod_spec(gate, tm, seq_len)],
        out_specs=row,
        out_shape=jax.ShapeDtypeStruct((t, D_MODEL), f32),
        name="out_proj",
    )(a, w, x, gate)


def _gla_kernel(q_ref, k_ref, v_ref, g_ref, lr_ref, wg_ref, bg_ref, on_ref, s0_ref,
                o_ref, so_ref, s_scr, *, bb, chunk, valid):
    c = pl.program_id(1)

    @pl.when(c == 0)
    def _():
        s_scr[...] = s0_ref[...]

    rows = lax.broadcasted_iota(jnp.int32, (chunk, chunk), 0)
    cols = lax.broadcasted_iota(jnp.int32, (chunk, chunk), 1)
    causal = rows >= cols
    tri = causal.astype(f32)
    ones = jnp.ones((chunk, GLA_HDV), f32)
    nt = (((1,), (1,)), ((), ()))
    tn = (((0,), (0,)), ((), ()))
    for bi in range(bb):
        gate_in = jnp.dot(lr_ref[bi].astype(bf16), wg_ref[...], preferred_element_type=f32) + bg_ref[...]
        log_a = jax.nn.log_sigmoid(gate_in) / GLA_TAU
        if valid < chunk:
            log_a = jnp.where(lax.broadcasted_iota(jnp.int32, log_a.shape, 0) < valid, log_a, 0.0)
        for h in range(GLA_HEADS):
            ks = slice(h * GLA_HDK, (h + 1) * GLA_HDK)
            vs = slice(h * GLA_HDV, (h + 1) * GLA_HDV)
            la = log_a[:, ks]
            b = jnp.dot(tri, la, precision=HIGHEST, preferred_element_type=f32)
            q = q_ref[bi, :, ks] * (GLA_HDK ** -0.5)
            k = k_ref[bi, :, ks]
            v = v_ref[bi, :, vs].astype(bf16)
            qd = (q * jnp.exp(b)).astype(bf16)
            kd = (k * jnp.exp(-b)).astype(bf16)
            scores = lax.dot_general(qd, kd, nt, preferred_element_type=f32)
            scores = jnp.where(causal, scores, 0.0).astype(bf16)
            s_old = s_scr[bi, h]
            o = (jnp.dot(scores, v, preferred_element_type=f32)
                 + jnp.dot(qd, s_old.astype(bf16), preferred_element_type=f32))
            b_end = b[chunk - 1:chunk, :]
            k_end = (k * jnp.exp(b_end - b)).astype(bf16)
            decay = jnp.exp(lax.dot_general(la, ones, tn, precision=HIGHEST, preferred_element_type=f32))
            s_scr[bi, h] = s_old * decay + lax.dot_general(k_end, v, tn, preferred_element_type=f32)
            on = _rms(o, on_ref[...])
            o_ref[bi, :, vs] = (on * jax.nn.silu(g_ref[bi, :, vs])).astype(o_ref.dtype)

    @pl.when(c == pl.num_programs(1) - 1)
    def _():
        so_ref[...] = s_scr[...]


def gla_mix(y, lr, w_gate2p, b_gate, onorm_g, s0, *, bb, chunk, valid):
    b, seq, _ = y.shape
    grid = (b // bb, seq // chunk)

    def col(width, blk):
        return pl.BlockSpec((bb, chunk, width), lambda i, c: (i, c, blk))

    st = pl.BlockSpec((bb, GLA_HEADS, GLA_HDK, GLA_HDV), lambda i, c: (i, 0, 0, 0))
    return pl.pallas_call(
        functools.partial(_gla_kernel, bb=bb, chunk=chunk, valid=valid),
        grid=grid,
        in_specs=[col(GLA_DK, 0), col(GLA_DK, 1), col(GLA_DV, 1), col(GLA_DV, 2),
                  pl.BlockSpec((bb, chunk, LANES), lambda i, c: (i, c, 0)),
                  pl.BlockSpec((LANES, GLA_DK), lambda i, c: (0, 0)),
                  pl.BlockSpec((1, GLA_DK), lambda i, c: (0, 0)),
                  pl.BlockSpec((1, GLA_HDV), lambda i, c: (0, 0)),
                  st],
        out_specs=[pl.BlockSpec((bb, chunk, GLA_DV), lambda i, c: (i, c, 0)), st],
        out_shape=[jax.ShapeDtypeStruct((b, seq, GLA_DV), bf16),
                   jax.ShapeDtypeStruct(s0.shape, f32)],
        scratch_shapes=[pltpu.VMEM((bb, GLA_HEADS, GLA_HDK, GLA_HDV), f32)],
        compiler_params=pltpu.CompilerParams(dimension_semantics=("parallel", "arbitrary")),
        name="gla_mix",
    )(y, y, y, y, lr, w_gate2p, b_gate.reshape(1, GLA_DK), onorm_g.reshape(1, GLA_HDV), s0)


def _rg_gates(xconv, wa_ref, ba_ref, wx_ref, bx_ref, lam_ref):
    xb = xconv.astype(bf16)
    gate_r = jax.nn.sigmoid(jnp.dot(xb, wa_ref[...], preferred_element_type=f32) + ba_ref[...])
    gate_i = jax.nn.sigmoid(jnp.dot(xb, wx_ref[...], preferred_element_type=f32) + bx_ref[...])
    log_a = -RG_C * gate_r * jax.nn.softplus(-lam_ref[...])
    a = jnp.exp(log_a)
    mult = jnp.sqrt(-jnp.tanh(log_a) * (a * a + 1.0))
    return a, mult, gate_i


def _rg_prefill_kernel(y_ref, xb_ref, h0_ref, c0_ref, cw_ref, cb_ref, wa_ref, ba_ref, wx_ref, bx_ref,
                       lam_ref, o_ref, ho_ref, co_ref, buf, h_scr, *, tl, start):
    t = pl.program_id(1)
    top = 8

    @pl.when(t == 0)
    def _():
        h_scr[...] = h0_ref[0]
        buf[top - (CONV_W - 1):top, :] = c0_ref[0]

    x = xb_ref[0]
    buf[top:top + tl, :] = x
    xconv = cb_ref[...] + sum(buf[top - (CONV_W - 1) + j:top - (CONV_W - 1) + j + tl, :] * cw_ref[j:j + 1, :]
                              for j in range(CONV_W))
    a, mult, gate_i = _rg_gates(xconv, wa_ref, ba_ref, wx_ref, bx_ref, lam_ref)
    row = lax.broadcasted_iota(jnp.int32, (tl, D_RNN), 0)
    reset = (row + t * tl + start) == 0
    a = jnp.where(reset, 0.0, a)
    mult = jnp.where(reset, 1.0, mult)
    b = mult * gate_i * xconv
    s = 1
    while s < tl:
        keep = row >= s
        b = jnp.where(keep, a * pltpu.roll(b, s, 0) + b, b)
        a = jnp.where(keep, a * pltpu.roll(a, s, 0), a)
        s *= 2
    hs = a * h_scr[...] + b
    o_ref[0] = (hs * jax.nn.gelu(y_ref[0])).astype(o_ref.dtype)
    h_scr[...] = hs[tl - 1:tl, :]
    buf[top - (CONV_W - 1):top, :] = x[tl - (CONV_W - 1):tl, :]

    @pl.when(t == pl.num_programs(1) - 1)
    def _():
        ho_ref[0] = hs[tl - 1:tl, :]
        co_ref[0] = x[tl - (CONV_W - 1):tl, :]


def _rg_vec(v):
    return v.reshape(1, D_RNN)


def rglru_prefill(y, h0, conv0, conv_w, conv_b, wa, ba, wx, bx, lam, *, start, tl):
    b, seq, _ = y.shape
    assert seq % tl == 0 and tl >= CONV_W - 1
    vec = pl.BlockSpec((1, D_RNN), lambda i, t: (0, 0))
    mat = pl.BlockSpec((D_RNN, D_RNN), lambda i, t: (0, 0))
    return pl.pallas_call(
        functools.partial(_rg_prefill_kernel, tl=tl, start=start),
        grid=(b, seq // tl),
        in_specs=[pl.BlockSpec((1, tl, D_RNN), lambda i, t: (i, t, 0)),
                  pl.BlockSpec((1, tl, D_RNN), lambda i, t: (i, t, 1)),
                  pl.BlockSpec((1, 1, D_RNN), lambda i, t: (i, 0, 0)),
                  pl.BlockSpec((1, CONV_W - 1, D_RNN), lambda i, t: (i, 0, 0)),
                  pl.BlockSpec((CONV_W, D_RNN), lambda i, t: (0, 0)),
                  vec, mat, vec, mat, vec, vec],
        out_specs=[pl.BlockSpec((1, tl, D_RNN), lambda i, t: (i, t, 0)),
                   pl.BlockSpec((1, 1, D_RNN), lambda i, t: (i, 0, 0)),
                   pl.BlockSpec((1, CONV_W - 1, D_RNN), lambda i, t: (i, 0, 0))],
        out_shape=[jax.ShapeDtypeStruct((b, seq, D_RNN), bf16),
                   jax.ShapeDtypeStruct((b, 1, D_RNN), f32),
                   jax.ShapeDtypeStruct((b, CONV_W - 1, D_RNN), f32)],
        scratch_shapes=[pltpu.VMEM((8 + tl, D_RNN), f32), pltpu.VMEM((1, D_RNN), f32)],
        compiler_params=pltpu.CompilerParams(dimension_semantics=("parallel", "arbitrary")),
        name="rglru_prefill",
    )(y, y, h0.reshape(b, 1, D_RNN), conv0, conv_w, _rg_vec(conv_b), wa, _rg_vec(ba), wx, _rg_vec(bx),
      _rg_vec(lam))


def _rg_decode_kernel(y_ref, xb_ref, h0_ref, c0_ref, cw_ref, cb_ref, wa_ref, ba_ref, wx_ref, bx_ref,
                      lam_ref, o_ref, ho_ref, *, steps, start):
    xc = [c0_ref[j] for j in range(CONV_W - 1)] + [xb_ref[s] for s in range(steps)]
    h = h0_ref[...]
    for s in range(steps):
        xconv = cb_ref[...] + sum(xc[s + j] * cw_ref[j:j + 1, :] for j in range(CONV_W))
        a, mult, gate_i = _rg_gates(xconv, wa_ref, ba_ref, wx_ref, bx_ref, lam_ref)
        if start + s == 0:
            a, mult = 0.0, 1.0
        h = a * h + mult * gate_i * xconv
        o_ref[s] = (h * jax.nn.gelu(y_ref[s])).astype(o_ref.dtype)
    ho_ref[...] = h


def rglru_decode(y, h0, conv0, conv_w, conv_b, wa, ba, wx, bx, lam, *, start):
    steps, b, _ = y.shape
    whole = lambda *shape: pl.BlockSpec(shape, lambda i: (0,) * len(shape))
    vec, mat = whole(1, D_RNN), whole(D_RNN, D_RNN)
    return pl.pallas_call(
        functools.partial(_rg_decode_kernel, steps=steps, start=start),
        grid=(1,),
        in_specs=[pl.BlockSpec((steps, b, D_RNN), lambda i: (0, 0, 0)),
                  pl.BlockSpec((steps, b, D_RNN), lambda i: (0, 0, 1)),
                  whole(b, D_RNN), whole(CONV_W - 1, b, D_RNN), whole(CONV_W, D_RNN),
                  vec, mat, vec, mat, vec, vec],
        out_specs=[whole(steps, b, D_RNN), whole(b, D_RNN)],
        out_shape=[jax.ShapeDtypeStruct((steps, b, D_RNN), bf16),
                   jax.ShapeDtypeStruct((b, D_RNN), f32)],
        name="rglru_decode",
    )(y, y, h0, conv0, conv_w, _rg_vec(conv_b), wa, _rg_vec(ba), wx, _rg_vec(bx), _rg_vec(lam))


def _top_rows(v, payload=None):
    nrows = v.shape[0]
    rowi = lax.broadcasted_iota(jnp.int32, v.shape, 0)
    vals, picks = [], []
    for _ in range(PEER_TOPK):
        m = jnp.max(v, axis=0, keepdims=True)
        i = jnp.min(jnp.where(v == m, rowi, nrows), axis=0, keepdims=True)
        hit = rowi == i
        vals.append(m)
        picks.append(i if payload is None else jnp.max(jnp.where(hit, payload, -1), axis=0, keepdims=True))
        v = jnp.where(hit, -jnp.inf, v)
    return jnp.concatenate(vals, axis=0), jnp.concatenate(picks, axis=0)


def _pair_rows(a0, a1, fill):
    sub = lax.broadcasted_iota(jnp.int32, (8,) + a0.shape[1:], 0)
    blocks = [a0[0:1] + a1]
    for a in range(1, 8):
        blk = a0[a:a + 1] + a1[0:8]
        nb = PEER_TOPK // (a + 1)
        if fill is not None and nb < 8:
            blk = jnp.where(sub < nb, blk, fill)
        blocks.append(blk)
    blocks.append(a0[8:16] + a1[0:1])
    return jnp.concatenate(blocks, axis=0)


def _peer_topk_kernel(q_ref, keys_ref, idx_ref, g_ref, *, tm):
    nt = (((1,), (1,)), ((), ()))
    idx_rows, g_rows = [], []
    for h in range(PEER_HEADS):
        top = []
        for p in range(2):
            col = (h * 2 + p) * PEER_DHALF
            qs = q_ref[:, col:col + PEER_DHALF].astype(bf16)
            s = lax.dot_general(keys_ref[h, p], qs, nt, preferred_element_type=f32)
            top.append(_top_rows(s))
        (s0, i0), (s1, i1) = top
        best, idx = _top_rows(_pair_rows(s0, s1, -jnp.inf), _pair_rows(i0 * PEER_NKEYS, i1, None))
        e = jnp.exp(best - best[0:1])
        g_rows.append(e / jnp.sum(e, axis=0, keepdims=True))
        idx_rows.append(idx.astype(f32))
    idx_ref[...] = jnp.concatenate(idx_rows, axis=0).T.astype(jnp.int32)
    g_ref[...] = jnp.concatenate(g_rows, axis=0).T


def peer_topk(q, keys, *, tm):
    t = q.shape[0]
    out = pl.BlockSpec((tm, PEER_ROWS), lambda i: (i, 0))
    return pl.pallas_call(
        functools.partial(_peer_topk_kernel, tm=tm),
        grid=(t // tm,),
        in_specs=[pl.BlockSpec((tm, 2 * PEER_HEADS * PEER_DHALF), lambda i: (i, 0)),
                  pl.BlockSpec((PEER_HEADS, 2, PEER_NKEYS, PEER_DHALF), lambda i: (0, 0, 0, 0))],
        out_specs=[out, out],
        out_shape=[jax.ShapeDtypeStruct((t, PEER_ROWS), jnp.int32),
                   jax.ShapeDtypeStruct((t, PEER_ROWS), f32)],
        name="peer_topk",
    )(q, keys)


def _peer_mid_kernel(dots_ref, g_ref, o_ref):
    o_ref[...] = _pack_twice(g_ref[...] * jax.nn.gelu(dots_ref[...]))


def peer_mid(dots, g, *, tm):
    t = dots.shape[0]
    blk = pl.BlockSpec((tm, PEER_ROWS), lambda i: (i, 0))
    return pl.pallas_call(
        _peer_mid_kernel,
        grid=(t // tm,),
        in_specs=[blk, blk],
        out_specs=blk,
        out_shape=jax.ShapeDtypeStruct((t, PEER_ROWS), jnp.int32),
        name="peer_mid",
    )(dots, g)


def _as_pairs(w):
    return plsc.bitcast(w, bf16)


def _sum4_split(p):
    s = plsc.bitcast((p[0] + p[1]) + (p[2] + p[3]), jnp.int32)
    return plsc.bitcast(s << 16, f32), plsc.bitcast(s & jnp.int32(-65536), f32)


def _sc_worker_base(tokens_per_worker):
    wid = lax.axis_index("s") * SC_CORES + lax.axis_index("c")
    return wid * tokens_per_worker


def _sc_kernel(body, t, out_width, scratch):
    mesh = plsc.VectorSubcoreMesh(core_axis_name="c", subcore_axis_name="s")
    return pl.kernel(
        body, mesh=mesh,
        out_type=jax.ShapeDtypeStruct((t, out_width), f32),
        scratch_types=scratch,
        compiler_params=pltpu.CompilerParams(needs_layout_passes=False),
    )


def _sc_tok_block(tokens_per_worker):
    tb = min(SC_TOK_BLOCK, tokens_per_worker)
    assert tokens_per_worker % tb == 0
    return tb


def _sc_stream(in_hbm, idx_hbm, tab_hbm, out_hbm, in_v, idx_v, rows_v, out_v, ring_sems, io_sems,
               compute, tokens_per_worker, tb):
    assert SC_CHUNKS == SC_RING
    base = _sc_worker_base(tokens_per_worker)
    nblk = tokens_per_worker // tb

    def stage(b, slot):
        rows = pl.ds(base + b * tb, tb)
        return (pltpu.make_async_copy(in_hbm.at[rows], in_v.at[slot], io_sems.at[0]),
                pltpu.make_async_copy(idx_hbm.at[rows], idx_v.at[slot], io_sems.at[1]))

    def writeback(b, slot):
        return pltpu.make_async_copy(out_v.at[slot], out_hbm.at[pl.ds(base + b * tb, tb)], io_sems.at[2 + slot])

    def gather(bslot, tok, part):
        return pltpu.make_async_copy(tab_hbm.at[idx_v.at[bslot, tok, part]], rows_v.at[part], ring_sems.at[part])

    for copy in stage(0, 0):
        copy.start()
    for copy in stage(0, 0):
        copy.wait()
    for part in range(SC_RING - 1):
        gather(0, 0, part).start()

    def turn(n, carry):
        b, tok = n // tb, n % tb
        bslot = b % 2
        more = b + 1 < nblk
        last_tok = tok == tb - 1

        @pl.when(tok == 0)
        def _():
            @pl.when(more)
            def _():
                for copy in stage(b + 1, 1 - bslot):
                    copy.start()

            @pl.when(b >= 2)
            def _():
                writeback(b - 2, bslot).wait()

        for part in range(SC_RING):
            ahead = (part + SC_RING - 1) % SC_RING
            if part == 0:
                gather(bslot, tok, ahead).start()
            else:
                @pl.when(jnp.logical_not(last_tok))
                def _():
                    gather(bslot, tok + 1, ahead).start()

                @pl.when(jnp.logical_and(last_tok, more))
                def _():
                    if part == 1:
                        for copy in stage(b + 1, 1 - bslot):
                            copy.wait()
                    gather(1 - bslot, 0, ahead).start()

            gather(bslot, tok, part).wait()
            compute(bslot, tok, part, rows_v.at[part])

        @pl.when(last_tok)
        def _():
            writeback(b, bslot).start()

        return carry

    lax.fori_loop(0, nblk * tb, turn, 0)
    for b in range(max(0, nblk - 2), nblk):
        writeback(b, b % 2).wait()


def sc_peer_dots(x, idx, tab):
    t = x.shape[0]
    tpw = t // SC_WORKERS
    tb = _sc_tok_block(tpw)
    rpg = 8
    quads = HALF_W // SC_LANES // 4

    def body(x_hbm, idx_hbm, tab_hbm, out_hbm, x_v, idx_v, rows_v, part_v, out_v, ring_sems, io_sems):
        lane = lax.iota(jnp.int32, SC_LANES)

        def compute(bslot, tok, part, rows):
            def group(g, c2):
                def words(j4, accs):
                    cols = [pl.ds((j4 * 4 + q) * SC_LANES, SC_LANES) for q in range(4)]
                    xs = [_as_pairs(x_v[bslot, tok, c]) for c in cols]
                    new = []
                    for r in range(rpg):
                        lo, hi = _sum4_split([_as_pairs(rows[g * rpg + r, c]) * xq for c, xq in zip(cols, xs)])
                        new.append(accs[r] + (lo + hi))
                    return tuple(new)

                accs = lax.fori_loop(0, quads, words, tuple(jnp.zeros((SC_LANES,), f32) for _ in range(rpg)))
                for r in range(rpg):
                    part_v[pl.ds((g * rpg + r) * SC_LANES, SC_LANES)] = accs[r]
                return c2

            lax.fori_loop(0, SC_CHUNK_ROWS // rpg, group, 0)

            def fold(rb, c3):
                first = rb * (SC_LANES * SC_LANES) + lane * SC_LANES
                s = plsc.load_gather(part_v, [first])
                for c in range(1, SC_LANES):
                    s = s + plsc.load_gather(part_v, [first + c])
                out_v[bslot, tok, pl.ds(part * SC_CHUNK_ROWS + rb * SC_LANES, SC_LANES)] = s
                return c3

            lax.fori_loop(0, SC_CHUNK_ROWS // SC_LANES, fold, 0)

        _sc_stream(x_hbm, idx_hbm, tab_hbm, out_hbm, x_v, idx_v, rows_v, out_v, ring_sems, io_sems,
                   compute, tpw, tb)

    scratch = [pltpu.VMEM((2, tb, HALF_W), jnp.int32),
               pltpu.VMEM((2, tb, SC_CHUNKS, SC_CHUNK_ROWS), jnp.int32),
               pltpu.VMEM((SC_RING, SC_CHUNK_ROWS, HALF_W), jnp.int32),
               pltpu.VMEM((SC_CHUNK_ROWS * SC_LANES,), f32),
               pltpu.VMEM((2, tb, PEER_ROWS), f32),
               pltpu.SemaphoreType.DMA((SC_RING,)),
               pltpu.SemaphoreType.DMA((4,))]
    return _sc_kernel(body, t, PEER_ROWS, scratch)(x, idx, tab)


def sc_peer_mix(wgt, idx, tab):
    t = wgt.shape[0]
    tpw = t // SC_WORKERS
    tb = _sc_tok_block(tpw)
    wpg = 8
    ngroups = HALF_W // SC_LANES // wpg

    def body(w_hbm, idx_hbm, tab_hbm, out_hbm, w_v, idx_v, rows_v, out_v, ring_sems, io_sems):
        zero = jnp.zeros((SC_LANES,), jnp.int32)

        def compute(bslot, tok, part, rows):
            def group(g, c2):
                def out_at(jj, hw):
                    return out_v.at[bslot, tok, pl.ds(hw * HALF_W + (g * wpg + jj) * SC_LANES, SC_LANES)]

                def quad(k4, accs):
                    wk = [_as_pairs(plsc.load_gather(w_v.at[bslot, tok],
                                                     [zero + (part * SC_CHUNK_ROWS + k4 * 4 + q)]))
                          for q in range(4)]
                    new = []
                    for jj in range(wpg):
                        col = pl.ds((g * wpg + jj) * SC_LANES, SC_LANES)
                        lo, hi = _sum4_split([_as_pairs(rows[k4 * 4 + q, col]) * wk[q] for q in range(4)])
                        new.append(accs[2 * jj] + lo)
                        new.append(accs[2 * jj + 1] + hi)
                    return tuple(new)

                if part == 0:
                    init = tuple(jnp.zeros((SC_LANES,), f32) for _ in range(2 * wpg))
                else:
                    init = tuple(out_at(jj, hw)[...] for jj in range(wpg) for hw in range(2))
                accs = lax.fori_loop(0, SC_CHUNK_ROWS // 4, quad, init)
                for jj in range(wpg):
                    for hw in range(2):
                        out_at(jj, hw)[...] = accs[2 * jj + hw]
                return c2

            lax.fori_loop(0, ngroups, group, 0)

        _sc_stream(w_hbm, idx_hbm, tab_hbm, out_hbm, w_v, idx_v, rows_v, out_v, ring_sems, io_sems,
                   compute, tpw, tb)

    scratch = [pltpu.VMEM((2, tb, PEER_ROWS), jnp.int32),
               pltpu.VMEM((2, tb, SC_CHUNKS, SC_CHUNK_ROWS), jnp.int32),
               pltpu.VMEM((SC_RING, SC_CHUNK_ROWS, HALF_W), jnp.int32),
               pltpu.VMEM((2, tb, D_MODEL), f32),
               pltpu.SemaphoreType.DMA((SC_RING,)),
               pltpu.SemaphoreType.DMA((4,))]
    return _sc_kernel(body, t, D_MODEL, scratch)(wgt, idx, tab)


def pack_table(tab):
    b = lax.bitcast_convert_type(tab.astype(bf16), jnp.uint16).astype(jnp.uint32)
    return lax.bitcast_convert_type(b[:, :HALF_W] | (b[:, HALF_W:] << 16), jnp.int32)


def peer_mix(hm, q, keys, u_packed, v_packed, *, tm):
    t = hm.shape[0]
    idx, g = peer_topk(q, keys, tm=tm)
    idx = idx.reshape(t, SC_CHUNKS, SC_CHUNK_ROWS)
    dots = sc_peer_dots(hm, idx, u_packed)
    wgt = peer_mid(dots, g, tm=tm)
    return sc_peer_mix(wgt, idx, v_packed)


def _block_diag(w):
    nb, n, _ = w.shape
    eye = jnp.eye(nb, dtype=w.dtype)
    return (eye[:, None, :, None] * w[:, :, None, :]).reshape(nb * n, nb * n)


def _prep_weights(p):
    w = {}
    w["gla_main"] = p["gla_w_in"][:, :, :2 * GLA_DK + 2 * GLA_DV].astype(bf16)
    w["gla_lr"] = jnp.pad(p["gla_w_in"][:, :, 2 * GLA_DK + 2 * GLA_DV:],
                          ((0, 0), (0, 0), (0, LANES - GLA_RANK))).astype(bf16)
    w["gla_gate2"] = jnp.pad(p["gla_w_gate2"], ((0, 0), (0, LANES - GLA_RANK), (0, 0))).astype(bf16)
    w["gla_out"] = p["gla_w_out"].astype(bf16)
    w["rg_in"] = p["rg_w_in"].astype(bf16)
    w["rg_wa"] = jax.vmap(_block_diag)(p["rg_w_a"]).astype(bf16)
    w["rg_wx"] = jax.vmap(_block_diag)(p["rg_w_x"]).astype(bf16)
    w["rg_out"] = p["rg_w_out"].astype(bf16)
    w["peer_q"] = p["peer_w_q"].astype(bf16)
    w["peer_keys"] = p["peer_sub_keys"].astype(bf16)
    w["peer_u"] = jax.vmap(pack_table)(p["peer_u"])
    w["peer_v"] = jax.vmap(pack_table)(p["peer_v"])
    return w


def _trunk(x, mod, start, gla_s0, rg_h0, rg_conv0, p, w, *, decode):
    b, seq, _ = x.shape
    t = b * seq
    tm = min(512, t if decode else seq)
    xt = x.reshape(t, D_MODEL)
    if decode:
        chunk = 16
        bb = 4
    else:
        chunk = min(GLA_CHUNK, seq)
        bb = 1
    delta = gate = None
    gla_new, rg_h_new, rg_conv_new = [], [], []
    for i in range(DEPTH):
        m = mod[i].reshape(b, 6, D_MODEL)
        if decode:
            mods = [jnp.repeat(m[:, j], seq, axis=0) for j in range(6)]
        else:
            mods = [m[:, j:j + 1] for j in range(6)]
        sh1, sc1, g1, sh2, sc2, g2 = mods
        kw = dict(seq_len=seq, tm=tm)
        j = i // 2
        if i % 2 == 0:
            if delta is None:
                (y,) = norm_proj(xt, p["norm_mix_g"][i], sc1, sh1, w["gla_main"][j], **kw)
            else:
                xt, y = norm_proj(xt, p["norm_mix_g"][i], sc1, sh1, w["gla_main"][j], delta=delta, gate=gate, **kw)
            (lr,) = norm_proj(xt, p["norm_mix_g"][i], sc1, sh1, w["gla_lr"][j], **kw)
            y3, lr3 = y.reshape(b, seq, -1), lr.reshape(b, seq, LANES)
            if decode:
                padding = ((0, 0), (0, chunk - seq), (0, 0))
                y3, lr3 = jnp.pad(y3, padding), jnp.pad(lr3, padding)
            o, s_new = gla_mix(y3, lr3, w["gla_gate2"][j], p["gla_b_gate"][j], p["gla_onorm_g"][j],
                               gla_s0[j], bb=bb, chunk=chunk, valid=min(seq, chunk))
            gla_new.append(s_new)
            o = o[:, :seq].reshape(t, GLA_DV)
            xt = out_proj(o, w["gla_out"][j], xt, g1, **kw)
        else:
            xt, y = norm_proj(xt, p["norm_mix_g"][i], sc1, sh1, w["rg_in"][j], delta=delta, gate=gate, **kw)
            rg_args = (p["rg_conv_w"][j], p["rg_conv_b"][j], w["rg_wa"][j], p["rg_b_a"][j], w["rg_wx"][j],
                       p["rg_b_x"][j], p["rg_lambda"][j])
            if decode:
                y3 = y.reshape(b, seq, 2 * D_RNN).transpose(1, 0, 2)
                o, h_new = rglru_decode(y3, rg_h0[j], rg_conv0[j].transpose(1, 0, 2), *rg_args, start=start)
                xbr = y.reshape(b, seq, 2 * D_RNN)[:, :, D_RNN:]
                conv_new = jnp.concatenate([rg_conv0[j], xbr], axis=1)[:, seq:]
                o = o.transpose(1, 0, 2).reshape(t, D_RNN)
            else:
                o, h_new, conv_new = rglru_prefill(y.reshape(b, seq, 2 * D_RNN), rg_h0[j], rg_conv0[j], *rg_args,
                                                   start=start, tl=min(256, seq))
                o, h_new = o.reshape(t, D_RNN), h_new.reshape(b, D_RNN)
            rg_h_new.append(h_new)
            rg_conv_new.append(conv_new)
            xt = out_proj(o, w["rg_out"][j], xt, g1, **kw)
        hm, q = norm_proj(xt, p["norm_ffn_g"][i], sc2, sh2, w["peer_q"][i], want_h="packed", **kw)
        delta = peer_mix(hm, q, w["peer_keys"][i], w["peer_u"][i], w["peer_v"][i], tm=min(256, t))
        gate = g2
    zeros = jnp.zeros_like(gate)
    xt, y = norm_proj(xt, p["final_norm_g"], zeros, zeros, delta=delta, gate=gate, want_h=True, seq_len=seq, tm=tm)
    return y.reshape(b, seq, D_MODEL), jnp.stack(gla_new), jnp.stack(rg_h_new), jnp.stack(rg_conv_new)


def kernel(x_prompt, x_sample, c_prompt, c_sample, state_gla, state_rglru_h, state_rglru_conv, ada_w, ada_b, norm_mix_g, norm_ffn_g, gla_w_in, gla_w_gate2, gla_b_gate, gla_onorm_g, gla_w_out, rg_w_in, rg_conv_w, rg_conv_b, rg_w_a, rg_b_a, rg_w_x, rg_b_x, rg_lambda, rg_w_out, peer_w_q, peer_sub_keys, peer_u, peer_v, final_norm_g):
    p = dict(norm_mix_g=norm_mix_g, norm_ffn_g=norm_ffn_g, gla_w_in=gla_w_in, gla_w_gate2=gla_w_gate2,
             gla_b_gate=gla_b_gate, gla_onorm_g=gla_onorm_g, gla_w_out=gla_w_out, rg_w_in=rg_w_in,
             rg_conv_w=rg_conv_w, rg_conv_b=rg_conv_b, rg_w_a=rg_w_a, rg_b_a=rg_b_a, rg_w_x=rg_w_x,
             rg_b_x=rg_b_x, rg_lambda=rg_lambda, rg_w_out=rg_w_out, peer_w_q=peer_w_q,
             peer_sub_keys=peer_sub_keys, peer_u=peer_u, peer_v=peer_v, final_norm_g=final_norm_g)
    w = _prep_weights(p)
    nb_p, nb_s = x_prompt.shape[0], x_sample.shape[0]
    n_gla, n_rg = state_gla.shape[0], state_rglru_h.shape[0]
    mod = adaln(jnp.concatenate([c_prompt, c_sample], axis=0), ada_w, ada_b)
    gla0 = jnp.zeros((n_gla, nb_p) + state_gla.shape[2:], f32)
    h0 = jnp.zeros((n_rg, nb_p, D_RNN), f32)
    conv0 = jnp.zeros((n_rg, nb_p, CONV_W - 1, D_RNN), f32)
    per = nb_p // PROMPT_STREAMS
    parts = []
    for s in range(PROMPT_STREAMS):
        rows = slice(s * per, (s + 1) * per)
        parts.append(_trunk(x_prompt[rows], mod[:, rows], 0, gla0[:, rows], h0[:, rows], conv0[:, rows], p, w,
                            decode=False))
    y_p = jnp.concatenate([o[0] for o in parts], axis=0)
    gla_p, h_p, conv_p = (jnp.concatenate([o[k] for o in parts], axis=1) for k in (1, 2, 3))
    y_s, gla_s, h_s, conv_s = _trunk(x_sample, mod[:, nb_p:], PAST_LEN, state_gla, state_rglru_h, state_rglru_conv,
                                     p, w, decode=True)
    return (y_p, y_s, gla_p, gla_s, h_p, h_s, conv_p, conv_s)
```

```python
import functools

import jax
import jax.numpy as jnp
from jax import lax
from jax.experimental import pallas as pl
from jax.experimental.pallas import tpu as pltpu
from jax.experimental.pallas import tpu_sc as plsc

f32 = jnp.float32
bf16 = jnp.bfloat16
HIGHEST = lax.Precision.HIGHEST

D_MODEL = 1024
DEPTH = 4
GLA_HEADS = 4
GLA_DK = 512
GLA_DV = 1024
GLA_HDK = 128
GLA_HDV = 256
GLA_RANK = 16
GLA_TAU = 16.0
GLA_CHUNK = 64
D_RNN = 1280
RG_BLOCKS = 8
RG_BLOCK = 160
CONV_W = 4
RG_C = 8.0
PEER_HEADS = 8
PEER_NKEYS = 128
PEER_DHALF = 128
PEER_TOPK = 16
PEER_ROWS = PEER_HEADS * PEER_TOPK
PAST_LEN = 16384
EPS = 1e-6

SC_CORES = 2
SC_SUBCORES = 16
SC_LANES = 16
SC_WORKERS = SC_CORES * SC_SUBCORES
LANES = 128
HALF_W = D_MODEL // 2
SC_CHUNK_ROWS = 32
SC_CHUNKS = PEER_ROWS // SC_CHUNK_ROWS
SC_RING = 4
SC_TOK_BLOCK = 16
PROMPT_STREAMS = 4


VMEM_CAP_BYTES = 60 * 2 ** 20


def _vmem_limit(block_bytes, temp_bytes):
    return min(VMEM_CAP_BYTES, 2 * block_bytes + temp_bytes)


def _rms(x, gain):
    return x * lax.rsqrt(jnp.mean(x * x, axis=-1, keepdims=True) + EPS) * gain


def _bf16_bits(v):
    return lax.bitcast_convert_type(v.astype(bf16).astype(f32), jnp.int32)


def _pack_halves(v):
    u = _bf16_bits(v)
    half = v.shape[-1] // 2
    return (u[:, half:] & jnp.int32(-65536)) | lax.shift_right_logical(u[:, :half], 16)


def _pack_twice(v):
    u = _bf16_bits(v)
    return u | lax.shift_right_logical(u, 16)


def _adaln_kernel(c_ref, w_ref, b_ref, o_ref):
    cond = jax.nn.silu(c_ref[...])
    o_ref[0] = jnp.dot(cond, w_ref[0], precision=HIGHEST, preferred_element_type=f32) + b_ref[0]


def adaln(c, ada_w, ada_b):
    bc = c.shape[0]
    tn = 1024
    return pl.pallas_call(
        _adaln_kernel,
        grid=(DEPTH, 6 * D_MODEL // tn),
        in_specs=[
            pl.BlockSpec((bc, D_MODEL), lambda i, j: (0, 0)),
            pl.BlockSpec((1, D_MODEL, tn), lambda i, j: (i, 0, j)),
            pl.BlockSpec((1, 1, tn), lambda i, j: (i, 0, j)),
        ],
        out_specs=pl.BlockSpec((1, bc, tn), lambda i, j: (i, 0, j)),
        out_shape=jax.ShapeDtypeStruct((DEPTH, bc, 6 * D_MODEL), f32),
        name="adaln",
    )(c, ada_w, ada_b.reshape(DEPTH, 1, 6 * D_MODEL))


def _proj_kernel(*refs, has_delta, has_w, want_h):
    it = iter(refs)
    x_ref = next(it)
    delta_ref = next(it) if has_delta else None
    gate_ref = next(it) if has_delta else None
    g_ref, sc_ref, sh_ref = next(it), next(it), next(it)
    w_ref = next(it) if has_w else None
    xo_ref = next(it) if has_delta else None
    h_ref = next(it) if want_h else None
    y_ref = next(it) if has_w else None
    hb_ref = next(it) if has_w else None

    def prologue():
        x = x_ref[...]
        if has_delta:
            x = x + gate_ref[...].reshape(-1, D_MODEL) * delta_ref[...]
            xo_ref[...] = x
        h = _rms(x, g_ref[...])
        h = h * (1.0 + sc_ref[...].reshape(-1, D_MODEL)) + sh_ref[...].reshape(-1, D_MODEL)
        if want_h == "packed":
            h_ref[...] = _pack_halves(h)
        elif want_h:
            h_ref[...] = h
        if has_w:
            hb_ref[...] = h.astype(bf16)

    if has_w:
        pl.when(pl.program_id(1) == 0)(prologue)
        y_ref[...] = jnp.dot(hb_ref[...], w_ref[...], preferred_element_type=f32)
    else:
        prologue()


def _mod_spec(mod, tm, seq_len):
    if mod.ndim == 3:
        per = seq_len // tm
        return pl.BlockSpec((1, 1, D_MODEL), lambda i, j: (i // per, 0, 0))
    return pl.BlockSpec((tm, D_MODEL), lambda i, j: (i, 0))


def _x_spec(x_rows, tm):
    first = 0 if x_rows is None else x_rows[0] // tm
    return pl.BlockSpec((tm, D_MODEL), lambda i, j: (i + first, 0))


def norm_proj(x, gain, scale, shift, w=None, *, delta=None, gate=None, want_h=False, seq_len, tm, tn=None,
              x_rows=None):
    t = x.shape[0] if x_rows is None else x_rows[1]
    has_delta, has_w = delta is not None, w is not None
    n = w.shape[1] if has_w else 0
    tn = (tn or n) if has_w else 0
    grid = (t // tm, n // tn if has_w else 1)
    row_bytes = tm * D_MODEL * 4
    block_bytes = row_bytes * (1 + 2 * has_delta + bool(want_h)) + D_MODEL * tn * 2 + tm * tn * 4
    temp_bytes = 4 * row_bytes + tm * tn * 4 + tm * D_MODEL * 2
    row = pl.BlockSpec((tm, D_MODEL), lambda i, j: (i, 0))
    args, specs = [x], [_x_spec(x_rows, tm)]
    if has_delta:
        args += [delta, gate]
        specs += [row, _mod_spec(gate, tm, seq_len)]
    args += [gain.reshape(1, D_MODEL), scale, shift]
    specs += [pl.BlockSpec((1, D_MODEL), lambda i, j: (0, 0)), _mod_spec(scale, tm, seq_len),
              _mod_spec(shift, tm, seq_len)]
    out_shapes, out_specs, scratch = [], [], []
    if has_delta:
        out_shapes.append(jax.ShapeDtypeStruct((t, D_MODEL), f32))
        out_specs.append(row)
    if want_h == "packed":
        out_shapes.append(jax.ShapeDtypeStruct((t, HALF_W), jnp.int32))
        out_specs.append(pl.BlockSpec((tm, HALF_W), lambda i, j: (i, 0)))
    elif want_h:
        out_shapes.append(jax.ShapeDtypeStruct((t, D_MODEL), f32))
        out_specs.append(row)
    if has_w:
        args.append(w)
        specs.append(pl.BlockSpec((D_MODEL, tn), lambda i, j: (0, j)))
        out_shapes.append(jax.ShapeDtypeStruct((t, n), f32))
        out_specs.append(pl.BlockSpec((tm, tn), lambda i, j: (i, j)))
        scratch.append(pltpu.VMEM((tm, D_MODEL), bf16))
    return pl.pallas_call(
        functools.partial(_proj_kernel, has_delta=has_delta, has_w=has_w, want_h=want_h),
        grid=grid, in_specs=specs, out_specs=out_specs, out_shape=out_shapes,
        scratch_shapes=scratch,
        compiler_params=pltpu.CompilerParams(dimension_semantics=("parallel", "arbitrary"),
                                             vmem_limit_bytes=_vmem_limit(block_bytes, temp_bytes)),
        name="norm_proj",
    )(*args)


def _out_proj_kernel(a_ref, w_ref, x_ref, gate_ref, o_ref):
    y = jnp.dot(a_ref[...], w_ref[...], preferred_element_type=f32)
    o_ref[...] = x_ref[...] + gate_ref[...].reshape(-1, D_MODEL) * y


def out_proj(a, w, x, gate, *, seq_len, tm, x_rows=None):
    t, k = a.shape
    row = pl.BlockSpec((tm, D_MODEL), lambda i, j: (i, 0))
    return pl.pallas_call(
        _out_proj_kernel,
        grid=(t // tm, 1),
        in_specs=[pl.BlockSpec((tm, k), lambda i, j: (i, 0)),
                  pl.BlockSpec((k, D_MODEL), lambda i, j: (0, 0)),
                  _x_spec(x_rows, tm), _mod_spec(gate, tm, seq_len)],
        out_specs=row,
        out_shape=jax.ShapeDtypeStruct((t, D_MODEL), f32),
        name="out_proj",
    )(a, w, x, gate)


def _gla_kernel(q_ref, k_ref, v_ref, g_ref, lr_ref, wg_ref, bg_ref, on_ref, s0_ref,
                o_ref, so_ref, s_scr, *, bb, chunk, valid):
    c = pl.program_id(1)

    @pl.when(c == 0)
    def _():
        s_scr[...] = s0_ref[...]

    rows = lax.broadcasted_iota(jnp.int32, (chunk, chunk), 0)
    cols = lax.broadcasted_iota(jnp.int32, (chunk, chunk), 1)
    causal = rows >= cols
    tri = causal.astype(f32)
    ones = jnp.ones((chunk, GLA_HDV), f32)
    nt = (((1,), (1,)), ((), ()))
    tn = (((0,), (0,)), ((), ()))
    for bi in range(bb):
        gate_in = jnp.dot(lr_ref[bi].astype(bf16), wg_ref[...], preferred_element_type=f32) + bg_ref[...]
        log_a = jax.nn.log_sigmoid(gate_in) / GLA_TAU
        if valid < chunk:
            log_a = jnp.where(lax.broadcasted_iota(jnp.int32, log_a.shape, 0) < valid, log_a, 0.0)
        for h in range(GLA_HEADS):
            ks = slice(h * GLA_HDK, (h + 1) * GLA_HDK)
            vs = slice(h * GLA_HDV, (h + 1) * GLA_HDV)
            la = log_a[:, ks]
            b = jnp.dot(tri, la, precision=HIGHEST, preferred_element_type=f32)
            q = q_ref[bi, :, ks] * (GLA_HDK ** -0.5)
            k = k_ref[bi, :, ks]
            v = v_ref[bi, :, vs].astype(bf16)
            qd = (q * jnp.exp(b)).astype(bf16)
            kd = (k * jnp.exp(-b)).astype(bf16)
            scores = lax.dot_general(qd, kd, nt, preferred_element_type=f32)
            scores = jnp.where(causal, scores, 0.0).astype(bf16)
            s_old = s_scr[bi, h]
            o = (jnp.dot(scores, v, preferred_element_type=f32)
                 + jnp.dot(qd, s_old.astype(bf16), preferred_element_type=f32))
            b_end = b[chunk - 1:chunk, :]
            k_end = (k * jnp.exp(b_end - b)).astype(bf16)
            decay = jnp.exp(lax.dot_general(la, ones, tn, precision=HIGHEST, preferred_element_type=f32))
            s_scr[bi, h] = s_old * decay + lax.dot_general(k_end, v, tn, preferred_element_type=f32)
            on = _rms(o, on_ref[...])
            o_ref[bi, :, vs] = (on * jax.nn.silu(g_ref[bi, :, vs])).astype(o_ref.dtype)

    @pl.when(c == pl.num_programs(1) - 1)
    def _():
        so_ref[...] = s_scr[...]


def gla_mix(y, lr, w_gate2p, b_gate, onorm_g, states, layer, *, bb, chunk, valid):
    b, seq, _ = y.shape
    grid = (b // bb, seq // chunk)

    def col(width, blk):
        return pl.BlockSpec((bb, chunk, width), lambda i, c: (i, c, blk))

    st = pl.BlockSpec((bb, GLA_HEADS, GLA_HDK, GLA_HDV), lambda i, c: (i, 0, 0, 0))
    st_in = pl.BlockSpec((None, bb, GLA_HEADS, GLA_HDK, GLA_HDV), lambda i, c: (layer, i, 0, 0, 0))
    return pl.pallas_call(
        functools.partial(_gla_kernel, bb=bb, chunk=chunk, valid=valid),
        grid=grid,
        in_specs=[col(GLA_DK, 0), col(GLA_DK, 1), col(GLA_DV, 1), col(GLA_DV, 2),
                  pl.BlockSpec((bb, chunk, LANES), lambda i, c: (i, c, 0)),
                  pl.BlockSpec((LANES, GLA_DK), lambda i, c: (0, 0)),
                  pl.BlockSpec((1, GLA_DK), lambda i, c: (0, 0)),
                  pl.BlockSpec((1, GLA_HDV), lambda i, c: (0, 0)),
                  st_in],
        out_specs=[pl.BlockSpec((bb, chunk, GLA_DV), lambda i, c: (i, c, 0)), st],
        out_shape=[jax.ShapeDtypeStruct((b, seq, GLA_DV), bf16),
                   jax.ShapeDtypeStruct(states.shape[1:], f32)],
        scratch_shapes=[pltpu.VMEM((bb, GLA_HEADS, GLA_HDK, GLA_HDV), f32)],
        compiler_params=pltpu.CompilerParams(dimension_semantics=("parallel", "arbitrary")),
        name="gla_mix",
    )(y, y, y, y, lr, w_gate2p, b_gate.reshape(1, GLA_DK), onorm_g.reshape(1, GLA_HDV), states)


def _rg_gates(xconv, wa_ref, ba_ref, wx_ref, bx_ref, lam_ref):
    xb = xconv.astype(bf16)
    gate_r = jax.nn.sigmoid(jnp.dot(xb, wa_ref[...], preferred_element_type=f32) + ba_ref[...])
    gate_i = jax.nn.sigmoid(jnp.dot(xb, wx_ref[...], preferred_element_type=f32) + bx_ref[...])
    log_a = -RG_C * gate_r * jax.nn.softplus(-lam_ref[...])
    a = jnp.exp(log_a)
    mult = jnp.sqrt(-jnp.tanh(log_a) * (a * a + 1.0))
    return a, mult, gate_i


def _rg_prefill_kernel(y_ref, xb_ref, h0_ref, c0_ref, cw_ref, cb_ref, wa_ref, ba_ref, wx_ref, bx_ref,
                       lam_ref, o_ref, ho_ref, co_ref, buf, h_scr, *, tl, start):
    t = pl.program_id(1)
    top = 8

    @pl.when(t == 0)
    def _():
        h_scr[...] = h0_ref[0]
        buf[top - (CONV_W - 1):top, :] = c0_ref[0]

    x = xb_ref[0]
    buf[top:top + tl, :] = x
    xconv = cb_ref[...] + sum(buf[top - (CONV_W - 1) + j:top - (CONV_W - 1) + j + tl, :] * cw_ref[j:j + 1, :]
                              for j in range(CONV_W))
    a, mult, gate_i = _rg_gates(xconv, wa_ref, ba_ref, wx_ref, bx_ref, lam_ref)
    row = lax.broadcasted_iota(jnp.int32, (tl, D_RNN), 0)
    reset = (row + t * tl + start) == 0
    a = jnp.where(reset, 0.0, a)
    mult = jnp.where(reset, 1.0, mult)
    b = mult * gate_i * xconv
    s = 1
    while s < tl:
        keep = row >= s
        b = jnp.where(keep, a * pltpu.roll(b, s, 0) + b, b)
        a = jnp.where(keep, a * pltpu.roll(a, s, 0), a)
        s *= 2
    hs = a * h_scr[...] + b
    o_ref[0] = (hs * jax.nn.gelu(y_ref[0])).astype(o_ref.dtype)
    h_scr[...] = hs[tl - 1:tl, :]
    buf[top - (CONV_W - 1):top, :] = x[tl - (CONV_W - 1):tl, :]

    @pl.when(t == pl.num_programs(1) - 1)
    def _():
        ho_ref[0] = hs[tl - 1:tl, :]
        co_ref[0] = x[tl - (CONV_W - 1):tl, :]


def _rg_vec(v):
    return v.reshape(1, D_RNN)


def rglru_prefill(y, h0, conv0, conv_w, conv_b, wa, ba, wx, bx, lam, *, start, tl):
    b, seq, _ = y.shape
    assert seq % tl == 0 and tl >= CONV_W - 1
    vec = pl.BlockSpec((1, D_RNN), lambda i, t: (0, 0))
    mat = pl.BlockSpec((D_RNN, D_RNN), lambda i, t: (0, 0))
    return pl.pallas_call(
        functools.partial(_rg_prefill_kernel, tl=tl, start=start),
        grid=(b, seq // tl),
        in_specs=[pl.BlockSpec((1, tl, D_RNN), lambda i, t: (i, t, 0)),
                  pl.BlockSpec((1, tl, D_RNN), lambda i, t: (i, t, 1)),
                  pl.BlockSpec((1, 1, D_RNN), lambda i, t: (i, 0, 0)),
                  pl.BlockSpec((1, CONV_W - 1, D_RNN), lambda i, t: (i, 0, 0)),
                  pl.BlockSpec((CONV_W, D_RNN), lambda i, t: (0, 0)),
                  vec, mat, vec, mat, vec, vec],
        out_specs=[pl.BlockSpec((1, tl, D_RNN), lambda i, t: (i, t, 0)),
                   pl.BlockSpec((1, 1, D_RNN), lambda i, t: (i, 0, 0)),
                   pl.BlockSpec((1, CONV_W - 1, D_RNN), lambda i, t: (i, 0, 0))],
        out_shape=[jax.ShapeDtypeStruct((b, seq, D_RNN), bf16),
                   jax.ShapeDtypeStruct((b, 1, D_RNN), f32),
                   jax.ShapeDtypeStruct((b, CONV_W - 1, D_RNN), f32)],
        scratch_shapes=[pltpu.VMEM((8 + tl, D_RNN), f32), pltpu.VMEM((1, D_RNN), f32)],
        compiler_params=pltpu.CompilerParams(dimension_semantics=("parallel", "arbitrary")),
        name="rglru_prefill",
    )(y, y, h0.reshape(b, 1, D_RNN), conv0, conv_w, _rg_vec(conv_b), wa, _rg_vec(ba), wx, _rg_vec(bx),
      _rg_vec(lam))


def _rg_decode_kernel(y_ref, xb_ref, h0_ref, c0_ref, cw_ref, cb_ref, wa_ref, ba_ref, wx_ref, bx_ref,
                      lam_ref, o_ref, ho_ref, *, steps, start):
    xc = [c0_ref[j] for j in range(CONV_W - 1)] + [xb_ref[s] for s in range(steps)]
    h = h0_ref[...]
    for s in range(steps):
        xconv = cb_ref[...] + sum(xc[s + j] * cw_ref[j:j + 1, :] for j in range(CONV_W))
        a, mult, gate_i = _rg_gates(xconv, wa_ref, ba_ref, wx_ref, bx_ref, lam_ref)
        if start + s == 0:
            a, mult = 0.0, 1.0
        h = a * h + mult * gate_i * xconv
        o_ref[s] = (h * jax.nn.gelu(y_ref[s])).astype(o_ref.dtype)
    ho_ref[...] = h


def rglru_decode(y, h0, conv0, conv_w, conv_b, wa, ba, wx, bx, lam, *, start):
    steps, b, _ = y.shape
    whole = lambda *shape: pl.BlockSpec(shape, lambda i: (0,) * len(shape))
    vec, mat = whole(1, D_RNN), whole(D_RNN, D_RNN)
    return pl.pallas_call(
        functools.partial(_rg_decode_kernel, steps=steps, start=start),
        grid=(1,),
        in_specs=[pl.BlockSpec((steps, b, D_RNN), lambda i: (0, 0, 0)),
                  pl.BlockSpec((steps, b, D_RNN), lambda i: (0, 0, 1)),
                  whole(b, D_RNN), whole(CONV_W - 1, b, D_RNN), whole(CONV_W, D_RNN),
                  vec, mat, vec, mat, vec, vec],
        out_specs=[whole(steps, b, D_RNN), whole(b, D_RNN)],
        out_shape=[jax.ShapeDtypeStruct((steps, b, D_RNN), bf16),
                   jax.ShapeDtypeStruct((b, D_RNN), f32)],
        name="rglru_decode",
    )(y, y, h0, conv0, conv_w, _rg_vec(conv_b), wa, _rg_vec(ba), wx, _rg_vec(bx), _rg_vec(lam))


def _top_rows(v, payload=None):
    nrows = v.shape[0]
    rowi = lax.broadcasted_iota(jnp.int32, v.shape, 0)
    vals, picks = [], []
    for _ in range(PEER_TOPK):
        m = jnp.max(v, axis=0, keepdims=True)
        i = jnp.min(jnp.where(v == m, rowi, nrows), axis=0, keepdims=True)
        hit = rowi == i
        vals.append(m)
        picks.append(i if payload is None else jnp.max(jnp.where(hit, payload, -1), axis=0, keepdims=True))
        v = jnp.where(hit, -jnp.inf, v)
    return jnp.concatenate(vals, axis=0), jnp.concatenate(picks, axis=0)


def _pair_rows(a0, a1, fill):
    sub = lax.broadcasted_iota(jnp.int32, (8,) + a0.shape[1:], 0)
    blocks = [a0[0:1] + a1]
    for a in range(1, 8):
        blk = a0[a:a + 1] + a1[0:8]
        nb = PEER_TOPK // (a + 1)
        if fill is not None and nb < 8:
            blk = jnp.where(sub < nb, blk, fill)
        blocks.append(blk)
    blocks.append(a0[8:16] + a1[0:1])
    return jnp.concatenate(blocks, axis=0)


def _peer_topk_kernel(q_ref, keys_ref, idx_ref, g_ref, *, tm):
    nt = (((1,), (1,)), ((), ()))
    idx_rows, g_rows = [], []
    for h in range(PEER_HEADS):
        top = []
        for p in range(2):
            col = (h * 2 + p) * PEER_DHALF
            qs = q_ref[:, col:col + PEER_DHALF].astype(bf16)
            s = lax.dot_general(keys_ref[h, p], qs, nt, preferred_element_type=f32)
            top.append(_top_rows(s))
        (s0, i0), (s1, i1) = top
        best, idx = _top_rows(_pair_rows(s0, s1, -jnp.inf), _pair_rows(i0 * PEER_NKEYS, i1, None))
        e = jnp.exp(best - best[0:1])
        g_rows.append(e / jnp.sum(e, axis=0, keepdims=True))
        idx_rows.append(idx.astype(f32))
    idx_ref[...] = jnp.concatenate(idx_rows, axis=0).T.astype(jnp.int32)
    g_ref[...] = jnp.concatenate(g_rows, axis=0).T


def peer_topk(q, keys, *, tm):
    t = q.shape[0]
    out = pl.BlockSpec((tm, PEER_ROWS), lambda i: (i, 0))
    return pl.pallas_call(
        functools.partial(_peer_topk_kernel, tm=tm),
        grid=(t // tm,),
        in_specs=[pl.BlockSpec((tm, 2 * PEER_HEADS * PEER_DHALF), lambda i: (i, 0)),
                  pl.BlockSpec((PEER_HEADS, 2, PEER_NKEYS, PEER_DHALF), lambda i: (0, 0, 0, 0))],
        out_specs=[out, out],
        out_shape=[jax.ShapeDtypeStruct((t, PEER_ROWS), jnp.int32),
                   jax.ShapeDtypeStruct((t, PEER_ROWS), f32)],
        name="peer_topk",
    )(q, keys)


def _peer_mid_kernel(dots_ref, g_ref, o_ref):
    o_ref[...] = _pack_twice(g_ref[...] * jax.nn.gelu(dots_ref[...]))


def peer_mid(dots, g, *, tm):
    t = dots.shape[0]
    blk = pl.BlockSpec((tm, PEER_ROWS), lambda i: (i, 0))
    return pl.pallas_call(
        _peer_mid_kernel,
        grid=(t // tm,),
        in_specs=[blk, blk],
        out_specs=blk,
        out_shape=jax.ShapeDtypeStruct((t, PEER_ROWS), jnp.int32),
        name="peer_mid",
    )(dots, g)


def _as_pairs(w):
    return plsc.bitcast(w, bf16)


def _sum4_split(p):
    s = plsc.bitcast((p[0] + p[1]) + (p[2] + p[3]), jnp.int32)
    return plsc.bitcast(s << 16, f32), plsc.bitcast(s & jnp.int32(-65536), f32)


def _sc_worker_base(tokens_per_worker):
    wid = lax.axis_index("s") * SC_CORES + lax.axis_index("c")
    return wid * tokens_per_worker


def _sc_kernel(body, t, out_width, scratch):
    mesh = plsc.VectorSubcoreMesh(core_axis_name="c", subcore_axis_name="s")
    return pl.kernel(
        body, mesh=mesh,
        out_type=jax.ShapeDtypeStruct((t, out_width), f32),
        scratch_types=scratch,
        compiler_params=pltpu.CompilerParams(needs_layout_passes=False),
    )


def _sc_tok_block(tokens_per_worker):
    tb = min(SC_TOK_BLOCK, tokens_per_worker)
    assert tokens_per_worker % tb == 0
    return tb


def _sc_stream(in_hbm, idx_hbm, tab_hbm, out_hbm, in_v, idx_v, rows_v, out_v, ring_sems, io_sems,
               compute, tokens_per_worker, tb):
    assert SC_CHUNKS == SC_RING
    base = _sc_worker_base(tokens_per_worker)
    nblk = tokens_per_worker // tb

    def stage(b, slot):
        rows = pl.ds(base + b * tb, tb)
        return (pltpu.make_async_copy(in_hbm.at[rows], in_v.at[slot], io_sems.at[0]),
                pltpu.make_async_copy(idx_hbm.at[rows], idx_v.at[slot], io_sems.at[1]))

    def writeback(b, slot):
        return pltpu.make_async_copy(out_v.at[slot], out_hbm.at[pl.ds(base + b * tb, tb)], io_sems.at[2 + slot])

    def gather(bslot, tok, part):
        return pltpu.make_async_copy(tab_hbm.at[idx_v.at[bslot, tok, part]], rows_v.at[part], ring_sems.at[part])

    for copy in stage(0, 0):
        copy.start()
    for copy in stage(0, 0):
        copy.wait()
    for part in range(SC_RING - 1):
        gather(0, 0, part).start()

    def turn(n, carry):
        b, tok = n // tb, n % tb
        bslot = b % 2
        more = b + 1 < nblk
        last_tok = tok == tb - 1

        @pl.when(tok == 0)
        def _():
            @pl.when(more)
            def _():
                for copy in stage(b + 1, 1 - bslot):
                    copy.start()

            @pl.when(b >= 2)
            def _():
                writeback(b - 2, bslot).wait()

        for part in range(SC_RING):
            ahead = (part + SC_RING - 1) % SC_RING
            if part == 0:
                gather(bslot, tok, ahead).start()
            else:
                @pl.when(jnp.logical_not(last_tok))
                def _():
                    gather(bslot, tok + 1, ahead).start()

                @pl.when(jnp.logical_and(last_tok, more))
                def _():
                    if part == 1:
                        for copy in stage(b + 1, 1 - bslot):
                            copy.wait()
                    gather(1 - bslot, 0, ahead).start()

            gather(bslot, tok, part).wait()
            compute(bslot, tok, part, rows_v.at[part])

        @pl.when(last_tok)
        def _():
            writeback(b, bslot).start()

        return carry

    lax.fori_loop(0, nblk * tb, turn, 0)
    for b in range(max(0, nblk - 2), nblk):
        writeback(b, b % 2).wait()


def sc_peer_dots(x, idx, tab):
    t = x.shape[0]
    tpw = t // SC_WORKERS
    tb = _sc_tok_block(tpw)
    rpg = 8
    quads = HALF_W // SC_LANES // 4

    def body(x_hbm, idx_hbm, tab_hbm, out_hbm, x_v, idx_v, rows_v, part_v, out_v, ring_sems, io_sems):
        lane = lax.iota(jnp.int32, SC_LANES)

        def compute(bslot, tok, part, rows):
            def group(g, c2):
                def words(j4, accs):
                    cols = [pl.ds((j4 * 4 + q) * SC_LANES, SC_LANES) for q in range(4)]
                    xs = [_as_pairs(x_v[bslot, tok, c]) for c in cols]
                    new = []
                    for r in range(rpg):
                        lo, hi = _sum4_split([_as_pairs(rows[g * rpg + r, c]) * xq for c, xq in zip(cols, xs)])
                        new.append(accs[r] + (lo + hi))
                    return tuple(new)

                accs = lax.fori_loop(0, quads, words, tuple(jnp.zeros((SC_LANES,), f32) for _ in range(rpg)))
                for r in range(rpg):
                    part_v[pl.ds((g * rpg + r) * SC_LANES, SC_LANES)] = accs[r]
                return c2

            lax.fori_loop(0, SC_CHUNK_ROWS // rpg, group, 0)

            def fold(rb, c3):
                first = rb * (SC_LANES * SC_LANES) + lane * SC_LANES
                s = plsc.load_gather(part_v, [first])
                for c in range(1, SC_LANES):
                    s = s + plsc.load_gather(part_v, [first + c])
                out_v[bslot, tok, pl.ds(part * SC_CHUNK_ROWS + rb * SC_LANES, SC_LANES)] = s
                return c3

            lax.fori_loop(0, SC_CHUNK_ROWS // SC_LANES, fold, 0)

        _sc_stream(x_hbm, idx_hbm, tab_hbm, out_hbm, x_v, idx_v, rows_v, out_v, ring_sems, io_sems,
                   compute, tpw, tb)

    scratch = [pltpu.VMEM((2, tb, HALF_W), jnp.int32),
               pltpu.VMEM((2, tb, SC_CHUNKS, SC_CHUNK_ROWS), jnp.int32),
               pltpu.VMEM((SC_RING, SC_CHUNK_ROWS, HALF_W), jnp.int32),
               pltpu.VMEM((SC_CHUNK_ROWS * SC_LANES,), f32),
               pltpu.VMEM((2, tb, PEER_ROWS), f32),
               pltpu.SemaphoreType.DMA((SC_RING,)),
               pltpu.SemaphoreType.DMA((4,))]
    return _sc_kernel(body, t, PEER_ROWS, scratch)(x, idx, tab)


def sc_peer_mix(wgt, idx, tab):
    t = wgt.shape[0]
    tpw = t // SC_WORKERS
    tb = _sc_tok_block(tpw)
    wpg = 8
    ngroups = HALF_W // SC_LANES // wpg

    def body(w_hbm, idx_hbm, tab_hbm, out_hbm, w_v, idx_v, rows_v, out_v, ring_sems, io_sems):
        zero = jnp.zeros((SC_LANES,), jnp.int32)

        def compute(bslot, tok, part, rows):
            def group(g, c2):
                def out_at(jj, hw):
                    return out_v.at[bslot, tok, pl.ds(hw * HALF_W + (g * wpg + jj) * SC_LANES, SC_LANES)]

                def quad(k4, accs):
                    wk = [_as_pairs(plsc.load_gather(w_v.at[bslot, tok],
                                                     [zero + (part * SC_CHUNK_ROWS + k4 * 4 + q)]))
                          for q in range(4)]
                    new = []
                    for jj in range(wpg):
                        col = pl.ds((g * wpg + jj) * SC_LANES, SC_LANES)
                        lo, hi = _sum4_split([_as_pairs(rows[k4 * 4 + q, col]) * wk[q] for q in range(4)])
                        new.append(accs[2 * jj] + lo)
                        new.append(accs[2 * jj + 1] + hi)
                    return tuple(new)

                if part == 0:
                    init = tuple(jnp.zeros((SC_LANES,), f32) for _ in range(2 * wpg))
                else:
                    init = tuple(out_at(jj, hw)[...] for jj in range(wpg) for hw in range(2))
                accs = lax.fori_loop(0, SC_CHUNK_ROWS // 4, quad, init)
                for jj in range(wpg):
                    for hw in range(2):
                        out_at(jj, hw)[...] = accs[2 * jj + hw]
                return c2

            lax.fori_loop(0, ngroups, group, 0)

        _sc_stream(w_hbm, idx_hbm, tab_hbm, out_hbm, w_v, idx_v, rows_v, out_v, ring_sems, io_sems,
                   compute, tpw, tb)

    scratch = [pltpu.VMEM((2, tb, PEER_ROWS), jnp.int32),
               pltpu.VMEM((2, tb, SC_CHUNKS, SC_CHUNK_ROWS), jnp.int32),
               pltpu.VMEM((SC_RING, SC_CHUNK_ROWS, HALF_W), jnp.int32),
               pltpu.VMEM((2, tb, D_MODEL), f32),
               pltpu.SemaphoreType.DMA((SC_RING,)),
               pltpu.SemaphoreType.DMA((4,))]
    return _sc_kernel(body, t, D_MODEL, scratch)(wgt, idx, tab)


def _pack_table_kernel(t_ref, o_ref):
    o_ref[...] = _pack_halves(t_ref[...])


def pack_table(tabs, layer, *, tn=512):
    _, n, d = tabs.shape
    return pl.pallas_call(
        _pack_table_kernel,
        grid=(n // tn,),
        in_specs=[pl.BlockSpec((None, tn, d), lambda i: (layer, i, 0))],
        out_specs=pl.BlockSpec((tn, d // 2), lambda i: (i, 0)),
        out_shape=jax.ShapeDtypeStruct((n, d // 2), jnp.int32),
        name="pack_table",
    )(tabs)


def peer_mix(hm, q, keys, u_packed, v_packed, *, tm):
    t = hm.shape[0]
    idx, g = peer_topk(q, keys, tm=tm)
    idx = idx.reshape(t, SC_CHUNKS, SC_CHUNK_ROWS)
    dots = sc_peer_dots(hm, idx, u_packed)
    wgt = peer_mid(dots, g, tm=tm)
    return sc_peer_mix(wgt, idx, v_packed)


def _block_diag(w):
    nb, n, _ = w.shape
    eye = jnp.eye(nb, dtype=w.dtype)
    return (eye[:, None, :, None] * w[:, :, None, :]).reshape(nb * n, nb * n)


def _split_gla_in_kernel(main_ref, tail_ref, wm_ref, wl_ref):
    wm_ref[...] = main_ref[...].astype(bf16)
    col = lax.broadcasted_iota(jnp.int32, tail_ref.shape, 1)
    wl_ref[...] = jnp.where(col < GLA_RANK, tail_ref[...], 0.0).astype(bf16)


def split_gla_in(w_in, layer, *, tk=256):
    _, d, n = w_in.shape
    qkvg = 2 * GLA_DK + 2 * GLA_DV
    assert n == qkvg + GLA_RANK and qkvg % LANES == 0
    return pl.pallas_call(
        _split_gla_in_kernel,
        grid=(d // tk,),
        in_specs=[pl.BlockSpec((None, tk, qkvg), lambda i: (layer, i, 0)),
                  pl.BlockSpec((None, tk, LANES), lambda i: (layer, i, qkvg // LANES))],
        out_specs=[pl.BlockSpec((tk, qkvg), lambda i: (i, 0)), pl.BlockSpec((tk, LANES), lambda i: (i, 0))],
        out_shape=[jax.ShapeDtypeStruct((d, qkvg), bf16), jax.ShapeDtypeStruct((d, LANES), bf16)],
        name="split_gla_in",
    )(w_in, w_in)


def _prep_weights(p):
    def per_layer(name, fn):
        return [fn(p[name][i]) for i in range(p[name].shape[0])]

    rank_pad = LANES - GLA_RANK
    w = {}
    gla_in = [split_gla_in(p["gla_w_in"], j) for j in range(p["gla_w_in"].shape[0])]
    w["gla_main"] = [a for a, _ in gla_in]
    w["gla_lr"] = [b for _, b in gla_in]
    w["gla_gate2"] = per_layer("gla_w_gate2", lambda a: jnp.pad(a, ((0, rank_pad), (0, 0))).astype(bf16))
    w["gla_out"] = per_layer("gla_w_out", lambda a: a.astype(bf16))
    w["rg_in"] = per_layer("rg_w_in", lambda a: a.astype(bf16))
    w["rg_wa"] = per_layer("rg_w_a", lambda a: _block_diag(a).astype(bf16))
    w["rg_wx"] = per_layer("rg_w_x", lambda a: _block_diag(a).astype(bf16))
    w["rg_out"] = per_layer("rg_w_out", lambda a: a.astype(bf16))
    w["peer_q"] = per_layer("peer_w_q", lambda a: a.astype(bf16))
    w["peer_keys"] = per_layer("peer_sub_keys", lambda a: a.astype(bf16))
    w["peer_u"] = [pack_table(p["peer_u"], i) for i in range(DEPTH)]
    w["peer_v"] = [pack_table(p["peer_v"], i) for i in range(DEPTH)]
    return w


def _trunk(x, rows, mod, start, gla_s0, rg_h0, rg_conv0, p, w, *, decode):
    seq = x.shape[1]
    b = rows.stop - rows.start
    t = b * seq
    tm = min(512, t if decode else seq)
    xt = x.reshape(-1, D_MODEL)
    x_rows = (rows.start * seq, t)
    if decode:
        chunk = 16
        bb = 4
    else:
        chunk = min(GLA_CHUNK, seq)
        bb = 1
    delta = gate = None
    gla_new, rg_h_new, rg_conv_new = [], [], []
    for i in range(DEPTH):
        m = mod[i].reshape(b, 6, D_MODEL)
        if decode:
            mods = [jnp.repeat(m[:, j], seq, axis=0) for j in range(6)]
        else:
            mods = [m[:, j:j + 1] for j in range(6)]
        sh1, sc1, g1, sh2, sc2, g2 = mods
        kw = dict(seq_len=seq, tm=tm)
        j = i // 2
        if i % 2 == 0:
            window = x_rows if delta is None else None
            if delta is None:
                (y,) = norm_proj(xt, p["norm_mix_g"][i], sc1, sh1, w["gla_main"][j], x_rows=window, **kw)
            else:
                xt, y = norm_proj(xt, p["norm_mix_g"][i], sc1, sh1, w["gla_main"][j], delta=delta, gate=gate, **kw)
            (lr,) = norm_proj(xt, p["norm_mix_g"][i], sc1, sh1, w["gla_lr"][j], x_rows=window, **kw)
            y3, lr3 = y.reshape(b, seq, -1), lr.reshape(b, seq, LANES)
            if decode:
                padding = ((0, 0), (0, chunk - seq), (0, 0))
                y3, lr3 = jnp.pad(y3, padding), jnp.pad(lr3, padding)
            o, s_new = gla_mix(y3, lr3, w["gla_gate2"][j], p["gla_b_gate"][j], p["gla_onorm_g"][j],
                               gla_s0, j, bb=bb, chunk=chunk, valid=min(seq, chunk))
            gla_new.append(s_new)
            o = o[:, :seq].reshape(t, GLA_DV)
            xt = out_proj(o, w["gla_out"][j], xt, g1, x_rows=window, **kw)
        else:
            xt, y = norm_proj(xt, p["norm_mix_g"][i], sc1, sh1, w["rg_in"][j], delta=delta, gate=gate, **kw)
            rg_args = (p["rg_conv_w"][j], p["rg_conv_b"][j], w["rg_wa"][j], p["rg_b_a"][j], w["rg_wx"][j],
                       p["rg_b_x"][j], p["rg_lambda"][j])
            if decode:
                y3 = y.reshape(b, seq, 2 * D_RNN).transpose(1, 0, 2)
                o, h_new = rglru_decode(y3, rg_h0[j], rg_conv0[j].transpose(1, 0, 2), *rg_args, start=start)
                xbr = y.reshape(b, seq, 2 * D_RNN)[:, :, D_RNN:]
                conv_new = jnp.concatenate([rg_conv0[j], xbr], axis=1)[:, seq:]
                o = o.transpose(1, 0, 2).reshape(t, D_RNN)
            else:
                o, h_new, conv_new = rglru_prefill(y.reshape(b, seq, 2 * D_RNN), rg_h0[j], rg_conv0[j], *rg_args,
                                                   start=start, tl=min(256, seq))
                o, h_new = o.reshape(t, D_RNN), h_new.reshape(b, D_RNN)
            rg_h_new.append(h_new)
            rg_conv_new.append(conv_new)
            xt = out_proj(o, w["rg_out"][j], xt, g1, **kw)
        hm, q = norm_proj(xt, p["norm_ffn_g"][i], sc2, sh2, w["peer_q"][i], want_h="packed", **kw)
        delta = peer_mix(hm, q, w["peer_keys"][i], w["peer_u"][i], w["peer_v"][i], tm=min(256, t))
        gate = g2
    zeros = jnp.zeros_like(gate)
    xt, y = norm_proj(xt, p["final_norm_g"], zeros, zeros, delta=delta, gate=gate, want_h=True, seq_len=seq, tm=tm)
    return y.reshape(b, seq, D_MODEL), jnp.stack(gla_new), jnp.stack(rg_h_new), jnp.stack(rg_conv_new)


def kernel(x_prompt, x_sample, c_prompt, c_sample, state_gla, state_rglru_h, state_rglru_conv, ada_w, ada_b, norm_mix_g, norm_ffn_g, gla_w_in, gla_w_gate2, gla_b_gate, gla_onorm_g, gla_w_out, rg_w_in, rg_conv_w, rg_conv_b, rg_w_a, rg_b_a, rg_w_x, rg_b_x, rg_lambda, rg_w_out, peer_w_q, peer_sub_keys, peer_u, peer_v, final_norm_g):
    p = dict(norm_mix_g=norm_mix_g, norm_ffn_g=norm_ffn_g, gla_w_in=gla_w_in, gla_w_gate2=gla_w_gate2,
             gla_b_gate=gla_b_gate, gla_onorm_g=gla_onorm_g, gla_w_out=gla_w_out, rg_w_in=rg_w_in,
             rg_conv_w=rg_conv_w, rg_conv_b=rg_conv_b, rg_w_a=rg_w_a, rg_b_a=rg_b_a, rg_w_x=rg_w_x,
             rg_b_x=rg_b_x, rg_lambda=rg_lambda, rg_w_out=rg_w_out, peer_w_q=peer_w_q,
             peer_sub_keys=peer_sub_keys, peer_u=peer_u, peer_v=peer_v, final_norm_g=final_norm_g)
    w = _prep_weights(p)
    nb_p, nb_s = x_prompt.shape[0], x_sample.shape[0]
    n_gla, n_rg = state_gla.shape[0], state_rglru_h.shape[0]
    mod = adaln(jnp.concatenate([c_prompt, c_sample], axis=0), ada_w, ada_b)
    per = nb_p // PROMPT_STREAMS
    gla0 = jnp.zeros((n_gla, per) + state_gla.shape[2:], f32)
    h0 = jnp.zeros((n_rg, per, D_RNN), f32)
    conv0 = jnp.zeros((n_rg, per, CONV_W - 1, D_RNN), f32)
    parts = []
    for s in range(PROMPT_STREAMS):
        rows = slice(s * per, (s + 1) * per)
        parts.append(_trunk(x_prompt, rows, mod[:, rows], 0, gla0, h0, conv0, p, w, decode=False))
    y_p = jnp.concatenate([o[0] for o in parts], axis=0)
    gla_p, h_p, conv_p = (jnp.concatenate([o[k] for o in parts], axis=1) for k in (1, 2, 3))
    y_s, gla_s, h_s, conv_s = _trunk(x_sample, slice(0, nb_s), mod[:, nb_p:], PAST_LEN, state_gla, state_rglru_h,
                                     state_rglru_conv, p, w, decode=True)
    return (y_p, y_s, gla_p, gla_s, h_p, h_s, conv_p, conv_s)
```

```python
import functools

import jax
import jax.numpy as jnp
from jax import lax
from jax.experimental import pallas as pl
from jax.experimental.pallas import tpu as pltpu
from jax.experimental.pallas import tpu_sc as plsc

f32 = jnp.float32
bf16 = jnp.bfloat16
HIGHEST = lax.Precision.HIGHEST

D_MODEL = 1024
DEPTH = 4
GLA_HEADS = 4
GLA_DK = 512
GLA_DV = 1024
GLA_HDK = 128
GLA_HDV = 256
GLA_RANK = 16
GLA_TAU = 16.0
GLA_CHUNK = 64
D_RNN = 1280
CONV_W = 4
RG_C = 8.0
PEER_HEADS = 8
PEER_NKEYS = 128
PEER_DHALF = 128
PEER_TOPK = 16
PEER_ROWS = PEER_HEADS * PEER_TOPK
PAST_LEN = 16384
EPS = 1e-6

SC_CORES = 2
SC_SUBCORES = 16
SC_LANES = 16
SC_WORKERS = SC_CORES * SC_SUBCORES
LANES = 128
HALF_W = D_MODEL // 2
SC_CHUNK_ROWS = 32
SC_CHUNKS = PEER_ROWS // SC_CHUNK_ROWS
SC_RING = 4
SC_TOK_BLOCK = 16
PROMPT_STREAMS = 4
PEER_TOKEN_TILE = 256
ROW_TILE = 512
RG_TIME_TILE = 256
ADALN_COL_TILE = 1024
GLA_DECODE_CHUNK = 16
GLA_DECODE_ROWS = 4


VMEM_CAP_BYTES = 60 * 2 ** 20


def _vmem_limit(block_bytes, temp_bytes):
    return min(VMEM_CAP_BYTES, 2 * block_bytes + temp_bytes)


def _rms(x, gain):
    return x * lax.rsqrt(jnp.mean(x * x, axis=-1, keepdims=True) + EPS) * gain


def _bf16_bits(v):
    return lax.bitcast_convert_type(v.astype(bf16).astype(f32), jnp.int32)


def _pack_halves(v):
    u = _bf16_bits(v)
    half = v.shape[-1] // 2
    return (u[:, half:] & jnp.int32(-65536)) | lax.shift_right_logical(u[:, :half], 16)


def _pack_twice(v):
    u = _bf16_bits(v)
    return u | lax.shift_right_logical(u, 16)


def _adaln_kernel(c_ref, w_ref, b_ref, o_ref):
    cond = jax.nn.silu(c_ref[...])
    o_ref[0] = jnp.dot(cond, w_ref[0], precision=HIGHEST, preferred_element_type=f32) + b_ref[0]


def adaln(c, ada_w, ada_b):
    bc = c.shape[0]
    tn = ADALN_COL_TILE
    return pl.pallas_call(
        _adaln_kernel,
        grid=(DEPTH, 6 * D_MODEL // tn),
        in_specs=[
            pl.BlockSpec((bc, D_MODEL), lambda i, j: (0, 0)),
            pl.BlockSpec((1, D_MODEL, tn), lambda i, j: (i, 0, j)),
            pl.BlockSpec((1, 1, tn), lambda i, j: (i, 0, j)),
        ],
        out_specs=pl.BlockSpec((1, bc, tn), lambda i, j: (i, 0, j)),
        out_shape=jax.ShapeDtypeStruct((DEPTH, bc, 6 * D_MODEL), f32),
        name="adaln",
    )(c, ada_w, ada_b.reshape(DEPTH, 1, 6 * D_MODEL))


def _proj_kernel(*refs, has_delta, has_w, want_h):
    it = iter(refs)
    x_ref = next(it)
    delta_ref = next(it) if has_delta else None
    gate_ref = next(it) if has_delta else None
    g_ref, sc_ref, sh_ref = next(it), next(it), next(it)
    w_ref = next(it) if has_w else None
    xo_ref = next(it) if has_delta else None
    h_ref = next(it) if want_h else None
    y_ref = next(it) if has_w else None
    hb_ref = next(it) if has_w else None

    def prologue():
        x = x_ref[...]
        if has_delta:
            x = x + gate_ref[...].reshape(-1, D_MODEL) * delta_ref[...]
            xo_ref[...] = x
        h = _rms(x, g_ref[...])
        h = h * (1.0 + sc_ref[...].reshape(-1, D_MODEL)) + sh_ref[...].reshape(-1, D_MODEL)
        if want_h == "packed":
            h_ref[...] = _pack_halves(h)
        elif want_h:
            h_ref[...] = h
        if has_w:
            hb_ref[...] = h.astype(bf16)

    if has_w:
        pl.when(pl.program_id(1) == 0)(prologue)
        y_ref[...] = jnp.dot(hb_ref[...], w_ref[...], preferred_element_type=f32)
    else:
        prologue()


def _mod_spec(mod, tm, seq_len):
    if mod.ndim == 3:
        per = seq_len // tm
        return pl.BlockSpec((1, 1, D_MODEL), lambda i, j: (i // per, 0, 0))
    return pl.BlockSpec((tm, D_MODEL), lambda i, j: (i, 0))


def _x_spec(x_rows, tm):
    first = 0 if x_rows is None else x_rows[0] // tm
    return pl.BlockSpec((tm, D_MODEL), lambda i, j: (i + first, 0))


def norm_proj(x, gain, scale, shift, w=None, *, delta=None, gate=None, want_h=False, seq_len, tm, tn=None,
              x_rows=None):
    t = x.shape[0] if x_rows is None else x_rows[1]
    has_delta, has_w = delta is not None, w is not None
    n = w.shape[1] if has_w else 0
    tn = (tn or n) if has_w else 0
    grid = (t // tm, n // tn if has_w else 1)
    row_bytes = tm * D_MODEL * 4
    block_bytes = row_bytes * (1 + 2 * has_delta + bool(want_h)) + D_MODEL * tn * 2 + tm * tn * 4
    temp_bytes = 4 * row_bytes + tm * tn * 4 + tm * D_MODEL * 2
    row = pl.BlockSpec((tm, D_MODEL), lambda i, j: (i, 0))
    args, specs = [x], [_x_spec(x_rows, tm)]
    if has_delta:
        args += [delta, gate]
        specs += [row, _mod_spec(gate, tm, seq_len)]
    args += [gain.reshape(1, D_MODEL), scale, shift]
    specs += [pl.BlockSpec((1, D_MODEL), lambda i, j: (0, 0)), _mod_spec(scale, tm, seq_len),
              _mod_spec(shift, tm, seq_len)]
    out_shapes, out_specs, scratch = [], [], []
    if has_delta:
        out_shapes.append(jax.ShapeDtypeStruct((t, D_MODEL), f32))
        out_specs.append(row)
    if want_h == "packed":
        out_shapes.append(jax.ShapeDtypeStruct((t, HALF_W), jnp.int32))
        out_specs.append(pl.BlockSpec((tm, HALF_W), lambda i, j: (i, 0)))
    elif want_h:
        out_shapes.append(jax.ShapeDtypeStruct((t, D_MODEL), f32))
        out_specs.append(row)
    if has_w:
        args.append(w)
        specs.append(pl.BlockSpec((D_MODEL, tn), lambda i, j: (0, j)))
        out_shapes.append(jax.ShapeDtypeStruct((t, n), f32))
        out_specs.append(pl.BlockSpec((tm, tn), lambda i, j: (i, j)))
        scratch.append(pltpu.VMEM((tm, D_MODEL), bf16))
    return pl.pallas_call(
        functools.partial(_proj_kernel, has_delta=has_delta, has_w=has_w, want_h=want_h),
        grid=grid, in_specs=specs, out_specs=out_specs, out_shape=out_shapes,
        scratch_shapes=scratch,
        compiler_params=pltpu.CompilerParams(dimension_semantics=("parallel", "arbitrary"),
                                             vmem_limit_bytes=_vmem_limit(block_bytes, temp_bytes)),
        name="norm_proj",
    )(*args)


def _out_proj_kernel(a_ref, w_ref, x_ref, gate_ref, o_ref):
    y = jnp.dot(a_ref[...], w_ref[...], preferred_element_type=f32)
    o_ref[...] = x_ref[...] + gate_ref[...].reshape(-1, D_MODEL) * y


def out_proj(a, w, x, gate, *, seq_len, tm, x_rows=None):
    t, k = a.shape
    row = pl.BlockSpec((tm, D_MODEL), lambda i, j: (i, 0))
    return pl.pallas_call(
        _out_proj_kernel,
        grid=(t // tm, 1),
        in_specs=[pl.BlockSpec((tm, k), lambda i, j: (i, 0)),
                  pl.BlockSpec((k, D_MODEL), lambda i, j: (0, 0)),
                  _x_spec(x_rows, tm), _mod_spec(gate, tm, seq_len)],
        out_specs=row,
        out_shape=jax.ShapeDtypeStruct((t, D_MODEL), f32),
        name="out_proj",
    )(a, w, x, gate)


def _gla_kernel(q_ref, k_ref, v_ref, g_ref, lr_ref, wg_ref, bg_ref, on_ref, s0_ref,
                o_ref, so_ref, s_scr, *, bb, chunk, valid):
    c = pl.program_id(1)

    @pl.when(c == 0)
    def _():
        s_scr[...] = s0_ref[...]

    rows = lax.broadcasted_iota(jnp.int32, (chunk, chunk), 0)
    cols = lax.broadcasted_iota(jnp.int32, (chunk, chunk), 1)
    causal = rows >= cols
    tri = causal.astype(f32)
    ones = jnp.ones((chunk, GLA_HDV), f32)
    nt = (((1,), (1,)), ((), ()))
    tn = (((0,), (0,)), ((), ()))
    for bi in range(bb):
        gate_in = jnp.dot(lr_ref[bi].astype(bf16), wg_ref[...], preferred_element_type=f32) + bg_ref[...]
        log_a = jax.nn.log_sigmoid(gate_in) / GLA_TAU
        if valid < chunk:
            log_a = jnp.where(lax.broadcasted_iota(jnp.int32, log_a.shape, 0) < valid, log_a, 0.0)
        for h in range(GLA_HEADS):
            ks = slice(h * GLA_HDK, (h + 1) * GLA_HDK)
            vs = slice(h * GLA_HDV, (h + 1) * GLA_HDV)
            la = log_a[:, ks]
            b = jnp.dot(tri, la, precision=HIGHEST, preferred_element_type=f32)
            q = q_ref[bi, :, ks] * (GLA_HDK ** -0.5)
            k = k_ref[bi, :, ks]
            v = v_ref[bi, :, vs].astype(bf16)
            qd = (q * jnp.exp(b)).astype(bf16)
            kd = (k * jnp.exp(-b)).astype(bf16)
            scores = lax.dot_general(qd, kd, nt, preferred_element_type=f32)
            scores = jnp.where(causal, scores, 0.0).astype(bf16)
            s_old = s_scr[bi, h]
            o = (jnp.dot(scores, v, preferred_element_type=f32)
                 + jnp.dot(qd, s_old.astype(bf16), preferred_element_type=f32))
            b_end = b[chunk - 1:chunk, :]
            k_end = (k * jnp.exp(b_end - b)).astype(bf16)
            decay = jnp.exp(lax.dot_general(la, ones, tn, precision=HIGHEST, preferred_element_type=f32))
            s_scr[bi, h] = s_old * decay + lax.dot_general(k_end, v, tn, preferred_element_type=f32)
            on = _rms(o, on_ref[...])
            o_ref[bi, :, vs] = (on * jax.nn.silu(g_ref[bi, :, vs])).astype(o_ref.dtype)

    @pl.when(c == pl.num_programs(1) - 1)
    def _():
        so_ref[...] = s_scr[...]


def gla_mix(y, lr, w_gate2p, b_gate, onorm_g, states, layer, *, bb, chunk, valid):
    b, seq, _ = y.shape
    grid = (b // bb, seq // chunk)

    def col(width, blk):
        return pl.BlockSpec((bb, chunk, width), lambda i, c: (i, c, blk))

    st = pl.BlockSpec((bb, GLA_HEADS, GLA_HDK, GLA_HDV), lambda i, c: (i, 0, 0, 0))
    st_in = pl.BlockSpec((None, bb, GLA_HEADS, GLA_HDK, GLA_HDV), lambda i, c: (layer, i, 0, 0, 0))
    return pl.pallas_call(
        functools.partial(_gla_kernel, bb=bb, chunk=chunk, valid=valid),
        grid=grid,
        in_specs=[col(GLA_DK, 0), col(GLA_DK, 1), col(GLA_DV, 1), col(GLA_DV, 2),
                  pl.BlockSpec((bb, chunk, LANES), lambda i, c: (i, c, 0)),
                  pl.BlockSpec((LANES, GLA_DK), lambda i, c: (0, 0)),
                  pl.BlockSpec((1, GLA_DK), lambda i, c: (0, 0)),
                  pl.BlockSpec((1, GLA_HDV), lambda i, c: (0, 0)),
                  st_in],
        out_specs=[pl.BlockSpec((bb, chunk, GLA_DV), lambda i, c: (i, c, 0)), st],
        out_shape=[jax.ShapeDtypeStruct((b, seq, GLA_DV), bf16),
                   jax.ShapeDtypeStruct(states.shape[1:], f32)],
        scratch_shapes=[pltpu.VMEM((bb, GLA_HEADS, GLA_HDK, GLA_HDV), f32)],
        compiler_params=pltpu.CompilerParams(dimension_semantics=("parallel", "arbitrary")),
        name="gla_mix",
    )(y, y, y, y, lr, w_gate2p, b_gate.reshape(1, GLA_DK), onorm_g.reshape(1, GLA_HDV), states)


def _rg_gates(xconv, wa_ref, ba_ref, wx_ref, bx_ref, lam_ref):
    xb = xconv.astype(bf16)
    gate_r = jax.nn.sigmoid(jnp.dot(xb, wa_ref[...], preferred_element_type=f32) + ba_ref[...])
    gate_i = jax.nn.sigmoid(jnp.dot(xb, wx_ref[...], preferred_element_type=f32) + bx_ref[...])
    log_a = -RG_C * gate_r * jax.nn.softplus(-lam_ref[...])
    a = jnp.exp(log_a)
    mult = jnp.sqrt(-jnp.tanh(log_a) * (a * a + 1.0))
    return a, mult, gate_i


def _rg_prefill_kernel(y_ref, xb_ref, h0_ref, c0_ref, cw_ref, cb_ref, wa_ref, ba_ref, wx_ref, bx_ref,
                       lam_ref, o_ref, ho_ref, co_ref, buf, h_scr, *, tl, start):
    t = pl.program_id(1)
    top = 8

    @pl.when(t == 0)
    def _():
        h_scr[...] = h0_ref[0]
        buf[top - (CONV_W - 1):top, :] = c0_ref[0]

    x = xb_ref[0]
    buf[top:top + tl, :] = x
    xconv = cb_ref[...] + sum(buf[top - (CONV_W - 1) + j:top - (CONV_W - 1) + j + tl, :] * cw_ref[j:j + 1, :]
                              for j in range(CONV_W))
    a, mult, gate_i = _rg_gates(xconv, wa_ref, ba_ref, wx_ref, bx_ref, lam_ref)
    row = lax.broadcasted_iota(jnp.int32, (tl, D_RNN), 0)
    reset = (row + t * tl + start) == 0
    a = jnp.where(reset, 0.0, a)
    mult = jnp.where(reset, 1.0, mult)
    b = mult * gate_i * xconv
    s = 1
    while s < tl:
        keep = row >= s
        b = jnp.where(keep, a * pltpu.roll(b, s, 0) + b, b)
        a = jnp.where(keep, a * pltpu.roll(a, s, 0), a)
        s *= 2
    hs = a * h_scr[...] + b
    o_ref[0] = (hs * jax.nn.gelu(y_ref[0])).astype(o_ref.dtype)
    h_scr[...] = hs[tl - 1:tl, :]
    buf[top - (CONV_W - 1):top, :] = x[tl - (CONV_W - 1):tl, :]

    @pl.when(t == pl.num_programs(1) - 1)
    def _():
        ho_ref[0] = hs[tl - 1:tl, :]
        co_ref[0] = x[tl - (CONV_W - 1):tl, :]


def _rg_vec(v):
    return v.reshape(1, D_RNN)


def rglru_prefill(y, h0, conv0, conv_w, conv_b, wa, ba, wx, bx, lam, *, start, tl):
    b, seq, _ = y.shape
    assert seq % tl == 0 and tl >= CONV_W - 1
    vec = pl.BlockSpec((1, D_RNN), lambda i, t: (0, 0))
    mat = pl.BlockSpec((D_RNN, D_RNN), lambda i, t: (0, 0))
    return pl.pallas_call(
        functools.partial(_rg_prefill_kernel, tl=tl, start=start),
        grid=(b, seq // tl),
        in_specs=[pl.BlockSpec((1, tl, D_RNN), lambda i, t: (i, t, 0)),
                  pl.BlockSpec((1, tl, D_RNN), lambda i, t: (i, t, 1)),
                  pl.BlockSpec((1, 1, D_RNN), lambda i, t: (i, 0, 0)),
                  pl.BlockSpec((1, CONV_W - 1, D_RNN), lambda i, t: (i, 0, 0)),
                  pl.BlockSpec((CONV_W, D_RNN), lambda i, t: (0, 0)),
                  vec, mat, vec, mat, vec, vec],
        out_specs=[pl.BlockSpec((1, tl, D_RNN), lambda i, t: (i, t, 0)),
                   pl.BlockSpec((1, 1, D_RNN), lambda i, t: (i, 0, 0)),
                   pl.BlockSpec((1, CONV_W - 1, D_RNN), lambda i, t: (i, 0, 0))],
        out_shape=[jax.ShapeDtypeStruct((b, seq, D_RNN), bf16),
                   jax.ShapeDtypeStruct((b, 1, D_RNN), f32),
                   jax.ShapeDtypeStruct((b, CONV_W - 1, D_RNN), f32)],
        scratch_shapes=[pltpu.VMEM((8 + tl, D_RNN), f32), pltpu.VMEM((1, D_RNN), f32)],
        compiler_params=pltpu.CompilerParams(dimension_semantics=("parallel", "arbitrary")),
        name="rglru_prefill",
    )(y, y, h0.reshape(b, 1, D_RNN), conv0, conv_w, _rg_vec(conv_b), wa, _rg_vec(ba), wx, _rg_vec(bx),
      _rg_vec(lam))


def _rg_decode_kernel(y_ref, xb_ref, h0_ref, c0_ref, cw_ref, cb_ref, wa_ref, ba_ref, wx_ref, bx_ref,
                      lam_ref, o_ref, ho_ref, *, steps, start):
    xc = [c0_ref[j] for j in range(CONV_W - 1)] + [xb_ref[s] for s in range(steps)]
    h = h0_ref[...]
    for s in range(steps):
        xconv = cb_ref[...] + sum(xc[s + j] * cw_ref[j:j + 1, :] for j in range(CONV_W))
        a, mult, gate_i = _rg_gates(xconv, wa_ref, ba_ref, wx_ref, bx_ref, lam_ref)
        if start + s == 0:
            a, mult = 0.0, 1.0
        h = a * h + mult * gate_i * xconv
        o_ref[s] = (h * jax.nn.gelu(y_ref[s])).astype(o_ref.dtype)
    ho_ref[...] = h


def rglru_decode(y, h0, conv0, conv_w, conv_b, wa, ba, wx, bx, lam, *, start):
    steps, b, _ = y.shape
    whole = lambda *shape: pl.BlockSpec(shape, lambda i: (0,) * len(shape))
    vec, mat = whole(1, D_RNN), whole(D_RNN, D_RNN)
    return pl.pallas_call(
        functools.partial(_rg_decode_kernel, steps=steps, start=start),
        grid=(1,),
        in_specs=[pl.BlockSpec((steps, b, D_RNN), lambda i: (0, 0, 0)),
                  pl.BlockSpec((steps, b, D_RNN), lambda i: (0, 0, 1)),
                  whole(b, D_RNN), whole(CONV_W - 1, b, D_RNN), whole(CONV_W, D_RNN),
                  vec, mat, vec, mat, vec, vec],
        out_specs=[whole(steps, b, D_RNN), whole(b, D_RNN)],
        out_shape=[jax.ShapeDtypeStruct((steps, b, D_RNN), bf16),
                   jax.ShapeDtypeStruct((b, D_RNN), f32)],
        name="rglru_decode",
    )(y, y, h0, conv0, conv_w, _rg_vec(conv_b), wa, _rg_vec(ba), wx, _rg_vec(bx), _rg_vec(lam))


def _top_rows(v, payload=None):
    nrows = v.shape[0]
    rowi = lax.broadcasted_iota(jnp.int32, v.shape, 0)
    vals, picks = [], []
    for _ in range(PEER_TOPK):
        m = jnp.max(v, axis=0, keepdims=True)
        i = jnp.min(jnp.where(v == m, rowi, nrows), axis=0, keepdims=True)
        hit = rowi == i
        vals.append(m)
        picks.append(i if payload is None else jnp.max(jnp.where(hit, payload, -1), axis=0, keepdims=True))
        v = jnp.where(hit, -jnp.inf, v)
    return jnp.concatenate(vals, axis=0), jnp.concatenate(picks, axis=0)


def _pair_rows(a0, a1, fill):
    sub = lax.broadcasted_iota(jnp.int32, (8,) + a0.shape[1:], 0)
    blocks = [a0[0:1] + a1]
    for a in range(1, 8):
        blk = a0[a:a + 1] + a1[0:8]
        nb = PEER_TOPK // (a + 1)
        if fill is not None and nb < 8:
            blk = jnp.where(sub < nb, blk, fill)
        blocks.append(blk)
    blocks.append(a0[8:16] + a1[0:1])
    return jnp.concatenate(blocks, axis=0)


def _peer_topk_kernel(q_ref, keys_ref, idx_ref, g_ref, *, tm):
    nt = (((1,), (1,)), ((), ()))
    idx_rows, g_rows = [], []
    for h in range(PEER_HEADS):
        top = []
        for p in range(2):
            col = (h * 2 + p) * PEER_DHALF
            qs = q_ref[:, col:col + PEER_DHALF].astype(bf16)
            s = lax.dot_general(keys_ref[h, p], qs, nt, preferred_element_type=f32)
            top.append(_top_rows(s))
        (s0, i0), (s1, i1) = top
        best, idx = _top_rows(_pair_rows(s0, s1, -jnp.inf), _pair_rows(i0 * PEER_NKEYS, i1, None))
        e = jnp.exp(best - best[0:1])
        g_rows.append(e / jnp.sum(e, axis=0, keepdims=True))
        idx_rows.append(idx.astype(f32))
    idx_ref[...] = jnp.concatenate(idx_rows, axis=0).T.astype(jnp.int32)
    g_ref[...] = jnp.concatenate(g_rows, axis=0).T


def peer_topk(q, keys, *, tm):
    t = q.shape[0]
    out = pl.BlockSpec((tm, PEER_ROWS), lambda i: (i, 0))
    return pl.pallas_call(
        functools.partial(_peer_topk_kernel, tm=tm),
        grid=(t // tm,),
        in_specs=[pl.BlockSpec((tm, 2 * PEER_HEADS * PEER_DHALF), lambda i: (i, 0)),
                  pl.BlockSpec((PEER_HEADS, 2, PEER_NKEYS, PEER_DHALF), lambda i: (0, 0, 0, 0))],
        out_specs=[out, out],
        out_shape=[jax.ShapeDtypeStruct((t, PEER_ROWS), jnp.int32),
                   jax.ShapeDtypeStruct((t, PEER_ROWS), f32)],
        name="peer_topk",
    )(q, keys)


def _peer_mid_kernel(dots_ref, g_ref, o_ref):
    o_ref[...] = _pack_twice(g_ref[...] * jax.nn.gelu(dots_ref[...]))


def peer_mid(dots, g, *, tm):
    t = dots.shape[0]
    blk = pl.BlockSpec((tm, PEER_ROWS), lambda i: (i, 0))
    return pl.pallas_call(
        _peer_mid_kernel,
        grid=(t // tm,),
        in_specs=[blk, blk],
        out_specs=blk,
        out_shape=jax.ShapeDtypeStruct((t, PEER_ROWS), jnp.int32),
        name="peer_mid",
    )(dots, g)


def _as_pairs(w):
    return plsc.bitcast(w, bf16)


def _sum4_split(p):
    s = plsc.bitcast((p[0] + p[1]) + (p[2] + p[3]), jnp.int32)
    return plsc.bitcast(s << 16, f32), plsc.bitcast(s & jnp.int32(-65536), f32)


def _sc_worker_base(tokens_per_worker):
    wid = lax.axis_index("s") * SC_CORES + lax.axis_index("c")
    return wid * tokens_per_worker


def _sc_kernel(body, t, out_width, scratch):
    mesh = plsc.VectorSubcoreMesh(core_axis_name="c", subcore_axis_name="s")
    return pl.kernel(
        body, mesh=mesh,
        out_type=jax.ShapeDtypeStruct((t, out_width), f32),
        scratch_types=scratch,
        compiler_params=pltpu.CompilerParams(needs_layout_passes=False),
    )


def _sc_tok_block(tokens_per_worker):
    tb = min(SC_TOK_BLOCK, tokens_per_worker)
    assert tokens_per_worker % tb == 0
    return tb


def _sc_stream(in_hbm, idx_hbm, tab_hbm, out_hbm, in_v, idx_v, rows_v, out_v, ring_sems, io_sems,
               compute, tokens_per_worker, tb):
    assert SC_CHUNKS == SC_RING
    base = _sc_worker_base(tokens_per_worker)
    nblk = tokens_per_worker // tb

    def stage(b, slot):
        rows = pl.ds(base + b * tb, tb)
        return (pltpu.make_async_copy(in_hbm.at[rows], in_v.at[slot], io_sems.at[0]),
                pltpu.make_async_copy(idx_hbm.at[rows], idx_v.at[slot], io_sems.at[1]))

    def writeback(b, slot):
        return pltpu.make_async_copy(out_v.at[slot], out_hbm.at[pl.ds(base + b * tb, tb)], io_sems.at[2 + slot])

    def gather(bslot, tok, part):
        return pltpu.make_async_copy(tab_hbm.at[idx_v.at[bslot, tok, part]], rows_v.at[part], ring_sems.at[part])

    for copy in stage(0, 0):
        copy.start()
    for copy in stage(0, 0):
        copy.wait()
    for part in range(SC_RING - 1):
        gather(0, 0, part).start()

    def turn(n, carry):
        b, tok = n // tb, n % tb
        bslot = b % 2
        more = b + 1 < nblk
        last_tok = tok == tb - 1

        @pl.when(tok == 0)
        def _():
            @pl.when(more)
            def _():
                for copy in stage(b + 1, 1 - bslot):
                    copy.start()

            @pl.when(b >= 2)
            def _():
                writeback(b - 2, bslot).wait()

        for part in range(SC_RING):
            ahead = (part + SC_RING - 1) % SC_RING
            if part == 0:
                gather(bslot, tok, ahead).start()
            else:
                @pl.when(jnp.logical_not(last_tok))
                def _():
                    gather(bslot, tok + 1, ahead).start()

                @pl.when(jnp.logical_and(last_tok, more))
                def _():
                    if part == 1:
                        for copy in stage(b + 1, 1 - bslot):
                            copy.wait()
                    gather(1 - bslot, 0, ahead).start()

            gather(bslot, tok, part).wait()
            compute(bslot, tok, part, rows_v.at[part])

        @pl.when(last_tok)
        def _():
            writeback(b, bslot).start()

        return carry

    lax.fori_loop(0, nblk * tb, turn, 0)
    for b in range(max(0, nblk - 2), nblk):
        writeback(b, b % 2).wait()


def sc_peer_dots(x, idx, tab):
    t = x.shape[0]
    tpw = t // SC_WORKERS
    tb = _sc_tok_block(tpw)
    rpg = 16
    quads = HALF_W // SC_LANES // 4

    def body(x_hbm, idx_hbm, tab_hbm, out_hbm, x_v, idx_v, rows_v, part_v, out_v, ring_sems, io_sems):
        lane = lax.iota(jnp.int32, SC_LANES)

        def compute(bslot, tok, part, rows):
            def group(g, c2):
                def words(j4, accs):
                    cols = [pl.ds((j4 * 4 + q) * SC_LANES, SC_LANES) for q in range(4)]
                    xs = [_as_pairs(x_v[bslot, tok, c]) for c in cols]
                    new = []
                    for r in range(rpg):
                        lo, hi = _sum4_split([_as_pairs(rows[g * rpg + r, c]) * xq for c, xq in zip(cols, xs)])
                        new.append(accs[r] + (lo + hi))
                    return tuple(new)

                accs = lax.fori_loop(0, quads, words, tuple(jnp.zeros((SC_LANES,), f32) for _ in range(rpg)))
                for r in range(rpg):
                    part_v[pl.ds((g * rpg + r) * SC_LANES, SC_LANES)] = accs[r]
                return c2

            lax.fori_loop(0, SC_CHUNK_ROWS // rpg, group, 0)

            def fold(rb, c3):
                first = rb * (SC_LANES * SC_LANES) + lane * SC_LANES
                cols = [plsc.load_gather(part_v, [first + c]) for c in range(SC_LANES)]
                while len(cols) > 1:
                    cols = [a + b for a, b in zip(cols[0::2], cols[1::2])]
                out_v[bslot, tok, pl.ds(part * SC_CHUNK_ROWS + rb * SC_LANES, SC_LANES)] = cols[0]
                return c3

            lax.fori_loop(0, SC_CHUNK_ROWS // SC_LANES, fold, 0)

        _sc_stream(x_hbm, idx_hbm, tab_hbm, out_hbm, x_v, idx_v, rows_v, out_v, ring_sems, io_sems,
                   compute, tpw, tb)

    scratch = [pltpu.VMEM((2, tb, HALF_W), jnp.int32),
               pltpu.VMEM((2, tb, SC_CHUNKS, SC_CHUNK_ROWS), jnp.int32),
               pltpu.VMEM((SC_RING, SC_CHUNK_ROWS, HALF_W), jnp.int32),
               pltpu.VMEM((SC_CHUNK_ROWS * SC_LANES,), f32),
               pltpu.VMEM((2, tb, PEER_ROWS), f32),
               pltpu.SemaphoreType.DMA((SC_RING,)),
               pltpu.SemaphoreType.DMA((4,))]
    return _sc_kernel(body, t, PEER_ROWS, scratch)(x, idx, tab)


def sc_peer_mix(wgt, idx, tab):
    t = wgt.shape[0]
    tpw = t // SC_WORKERS
    tb = _sc_tok_block(tpw)
    wpg = 16
    ngroups = HALF_W // SC_LANES // wpg

    def body(w_hbm, idx_hbm, tab_hbm, out_hbm, w_v, idx_v, rows_v, out_v, ring_sems, io_sems):
        zero = jnp.zeros((SC_LANES,), jnp.int32)

        def compute(bslot, tok, part, rows):
            def group(g, c2):
                def out_at(jj, hw):
                    return out_v.at[bslot, tok, pl.ds(hw * HALF_W + (g * wpg + jj) * SC_LANES, SC_LANES)]

                def quad(k4, accs):
                    wk = [_as_pairs(plsc.load_gather(w_v.at[bslot, tok],
                                                     [zero + (part * SC_CHUNK_ROWS + k4 * 4 + q)]))
                          for q in range(4)]
                    new = []
                    for jj in range(wpg):
                        col = pl.ds((g * wpg + jj) * SC_LANES, SC_LANES)
                        lo, hi = _sum4_split([_as_pairs(rows[k4 * 4 + q, col]) * wk[q] for q in range(4)])
                        new.append(accs[2 * jj] + lo)
                        new.append(accs[2 * jj + 1] + hi)
                    return tuple(new)

                if part == 0:
                    init = tuple(jnp.zeros((SC_LANES,), f32) for _ in range(2 * wpg))
                else:
                    init = tuple(out_at(jj, hw)[...] for jj in range(wpg) for hw in range(2))
                accs = lax.fori_loop(0, SC_CHUNK_ROWS // 4, quad, init)
                for jj in range(wpg):
                    for hw in range(2):
                        out_at(jj, hw)[...] = accs[2 * jj + hw]
                return c2

            lax.fori_loop(0, ngroups, group, 0)

        _sc_stream(w_hbm, idx_hbm, tab_hbm, out_hbm, w_v, idx_v, rows_v, out_v, ring_sems, io_sems,
                   compute, tpw, tb)

    scratch = [pltpu.VMEM((2, tb, PEER_ROWS), jnp.int32),
               pltpu.VMEM((2, tb, SC_CHUNKS, SC_CHUNK_ROWS), jnp.int32),
               pltpu.VMEM((SC_RING, SC_CHUNK_ROWS, HALF_W), jnp.int32),
               pltpu.VMEM((2, tb, D_MODEL), f32),
               pltpu.SemaphoreType.DMA((SC_RING,)),
               pltpu.SemaphoreType.DMA((4,))]
    return _sc_kernel(body, t, D_MODEL, scratch)(wgt, idx, tab)


def _pack_table_kernel(t_ref, o_ref):
    o_ref[...] = _pack_halves(t_ref[...])


def pack_table(tabs, layer, *, tn=512):
    _, n, d = tabs.shape
    return pl.pallas_call(
        _pack_table_kernel,
        grid=(n // tn,),
        in_specs=[pl.BlockSpec((None, tn, d), lambda i: (layer, i, 0))],
        out_specs=pl.BlockSpec((tn, d // 2), lambda i: (i, 0)),
        out_shape=jax.ShapeDtypeStruct((n, d // 2), jnp.int32),
        name="pack_table",
    )(tabs)


def peer_mix(hm, q, keys, u_packed, v_packed, *, tm):
    t = hm.shape[0]
    idx, g = peer_topk(q, keys, tm=tm)
    idx = idx.reshape(t, SC_CHUNKS, SC_CHUNK_ROWS)
    dots = sc_peer_dots(hm, idx, u_packed)
    wgt = peer_mid(dots, g, tm=tm)
    return sc_peer_mix(wgt, idx, v_packed)


def _block_diag(w):
    nb, n, _ = w.shape
    eye = jnp.eye(nb, dtype=w.dtype)
    return (eye[:, None, :, None] * w[:, :, None, :]).reshape(nb * n, nb * n)


def _split_gla_in_kernel(main_ref, tail_ref, wm_ref, wl_ref):
    wm_ref[...] = main_ref[...].astype(bf16)
    col = lax.broadcasted_iota(jnp.int32, tail_ref.shape, 1)
    wl_ref[...] = jnp.where(col < GLA_RANK, tail_ref[...], 0.0).astype(bf16)


def split_gla_in(w_in, layer, *, tk=256):
    _, d, n = w_in.shape
    qkvg = 2 * GLA_DK + 2 * GLA_DV
    assert n == qkvg + GLA_RANK and qkvg % LANES == 0
    return pl.pallas_call(
        _split_gla_in_kernel,
        grid=(d // tk,),
        in_specs=[pl.BlockSpec((None, tk, qkvg), lambda i: (layer, i, 0)),
                  pl.BlockSpec((None, tk, LANES), lambda i: (layer, i, qkvg // LANES))],
        out_specs=[pl.BlockSpec((tk, qkvg), lambda i: (i, 0)), pl.BlockSpec((tk, LANES), lambda i: (i, 0))],
        out_shape=[jax.ShapeDtypeStruct((d, qkvg), bf16), jax.ShapeDtypeStruct((d, LANES), bf16)],
        name="split_gla_in",
    )(w_in, w_in)


def _prep_weights(p):
    def per_layer(name, fn):
        return [fn(p[name][i]) for i in range(p[name].shape[0])]

    rank_pad = LANES - GLA_RANK
    w = {}
    gla_in = [split_gla_in(p["gla_w_in"], j) for j in range(p["gla_w_in"].shape[0])]
    w["gla_main"] = [a for a, _ in gla_in]
    w["gla_lr"] = [b for _, b in gla_in]
    w["gla_gate2"] = per_layer("gla_w_gate2", lambda a: jnp.pad(a, ((0, rank_pad), (0, 0))).astype(bf16))
    w["gla_out"] = per_layer("gla_w_out", lambda a: a.astype(bf16))
    w["rg_in"] = per_layer("rg_w_in", lambda a: a.astype(bf16))
    w["rg_wa"] = per_layer("rg_w_a", lambda a: _block_diag(a).astype(bf16))
    w["rg_wx"] = per_layer("rg_w_x", lambda a: _block_diag(a).astype(bf16))
    w["rg_out"] = per_layer("rg_w_out", lambda a: a.astype(bf16))
    w["peer_q"] = per_layer("peer_w_q", lambda a: a.astype(bf16))
    w["peer_keys"] = per_layer("peer_sub_keys", lambda a: a.astype(bf16))
    w["peer_u"] = [pack_table(p["peer_u"], i) for i in range(DEPTH)]
    w["peer_v"] = [pack_table(p["peer_v"], i) for i in range(DEPTH)]
    return w


def _trunk(x, rows, mod, start, gla_s0, rg_h0, rg_conv0, p, w, *, decode):
    seq = x.shape[1]
    b = rows.stop - rows.start
    t = b * seq
    tm = min(ROW_TILE, t if decode else seq)
    xt = x.reshape(-1, D_MODEL)
    x_rows = (rows.start * seq, t)
    if decode:
        chunk = GLA_DECODE_CHUNK
        bb = GLA_DECODE_ROWS
    else:
        chunk = min(GLA_CHUNK, seq)
        bb = 1
    delta = gate = None
    gla_new, rg_h_new, rg_conv_new = [], [], []
    for i in range(DEPTH):
        m = mod[i].reshape(b, 6, D_MODEL)
        if decode:
            mods = [jnp.repeat(m[:, j], seq, axis=0) for j in range(6)]
        else:
            mods = [m[:, j:j + 1] for j in range(6)]
        sh1, sc1, g1, sh2, sc2, g2 = mods
        kw = dict(seq_len=seq, tm=tm)
        j = i // 2
        if i % 2 == 0:
            window = x_rows if delta is None else None
            if delta is None:
                (y,) = norm_proj(xt, p["norm_mix_g"][i], sc1, sh1, w["gla_main"][j], x_rows=window, **kw)
            else:
                xt, y = norm_proj(xt, p["norm_mix_g"][i], sc1, sh1, w["gla_main"][j], delta=delta, gate=gate, **kw)
            (lr,) = norm_proj(xt, p["norm_mix_g"][i], sc1, sh1, w["gla_lr"][j], x_rows=window, **kw)
            y3, lr3 = y.reshape(b, seq, -1), lr.reshape(b, seq, LANES)
            if decode:
                padding = ((0, 0), (0, chunk - seq), (0, 0))
                y3, lr3 = jnp.pad(y3, padding), jnp.pad(lr3, padding)
            o, s_new = gla_mix(y3, lr3, w["gla_gate2"][j], p["gla_b_gate"][j], p["gla_onorm_g"][j],
                               gla_s0, j, bb=bb, chunk=chunk, valid=min(seq, chunk))
            gla_new.append(s_new)
            o = o[:, :seq].reshape(t, GLA_DV)
            xt = out_proj(o, w["gla_out"][j], xt, g1, x_rows=window, **kw)
        else:
            xt, y = norm_proj(xt, p["norm_mix_g"][i], sc1, sh1, w["rg_in"][j], delta=delta, gate=gate, **kw)
            rg_args = (p["rg_conv_w"][j], p["rg_conv_b"][j], w["rg_wa"][j], p["rg_b_a"][j], w["rg_wx"][j],
                       p["rg_b_x"][j], p["rg_lambda"][j])
            if decode:
                y3 = y.reshape(b, seq, 2 * D_RNN).transpose(1, 0, 2)
                o, h_new = rglru_decode(y3, rg_h0[j], rg_conv0[j].transpose(1, 0, 2), *rg_args, start=start)
                xbr = y.reshape(b, seq, 2 * D_RNN)[:, :, D_RNN:]
                conv_new = jnp.concatenate([rg_conv0[j], xbr], axis=1)[:, seq:]
                o = o.transpose(1, 0, 2).reshape(t, D_RNN)
            else:
                o, h_new, conv_new = rglru_prefill(y.reshape(b, seq, 2 * D_RNN), rg_h0[j], rg_conv0[j], *rg_args,
                                                   start=start, tl=min(RG_TIME_TILE, seq))
                o, h_new = o.reshape(t, D_RNN), h_new.reshape(b, D_RNN)
            rg_h_new.append(h_new)
            rg_conv_new.append(conv_new)
            xt = out_proj(o, w["rg_out"][j], xt, g1, **kw)
        hm, q = norm_proj(xt, p["norm_ffn_g"][i], sc2, sh2, w["peer_q"][i], want_h="packed", **kw)
        delta = peer_mix(hm, q, w["peer_keys"][i], w["peer_u"][i], w["peer_v"][i], tm=min(PEER_TOKEN_TILE, t))
        gate = g2
    zeros = jnp.zeros_like(gate)
    xt, y = norm_proj(xt, p["final_norm_g"], zeros, zeros, delta=delta, gate=gate, want_h=True, seq_len=seq, tm=tm)
    return y.reshape(b, seq, D_MODEL), jnp.stack(gla_new), jnp.stack(rg_h_new), jnp.stack(rg_conv_new)


def kernel(x_prompt, x_sample, c_prompt, c_sample, state_gla, state_rglru_h, state_rglru_conv, ada_w, ada_b, norm_mix_g, norm_ffn_g, gla_w_in, gla_w_gate2, gla_b_gate, gla_onorm_g, gla_w_out, rg_w_in, rg_conv_w, rg_conv_b, rg_w_a, rg_b_a, rg_w_x, rg_b_x, rg_lambda, rg_w_out, peer_w_q, peer_sub_keys, peer_u, peer_v, final_norm_g):
    p = dict(norm_mix_g=norm_mix_g, norm_ffn_g=norm_ffn_g, gla_w_in=gla_w_in, gla_w_gate2=gla_w_gate2,
             gla_b_gate=gla_b_gate, gla_onorm_g=gla_onorm_g, gla_w_out=gla_w_out, rg_w_in=rg_w_in,
             rg_conv_w=rg_conv_w, rg_conv_b=rg_conv_b, rg_w_a=rg_w_a, rg_b_a=rg_b_a, rg_w_x=rg_w_x,
             rg_b_x=rg_b_x, rg_lambda=rg_lambda, rg_w_out=rg_w_out, peer_w_q=peer_w_q,
             peer_sub_keys=peer_sub_keys, peer_u=peer_u, peer_v=peer_v, final_norm_g=final_norm_g)
    w = _prep_weights(p)
    nb_p, nb_s = x_prompt.shape[0], x_sample.shape[0]
    assert nb_p % PROMPT_STREAMS == 0
    n_gla, n_rg = state_gla.shape[0], state_rglru_h.shape[0]
    mod = adaln(jnp.concatenate([c_prompt, c_sample], axis=0), ada_w, ada_b)
    per = nb_p // PROMPT_STREAMS
    gla0 = jnp.zeros((n_gla, per) + state_gla.shape[2:], f32)
    h0 = jnp.zeros((n_rg, per, D_RNN), f32)
    conv0 = jnp.zeros((n_rg, per, CONV_W - 1, D_RNN), f32)
    parts = []
    for s in range(PROMPT_STREAMS):
        rows = slice(s * per, (s + 1) * per)
        parts.append(_trunk(x_prompt, rows, mod[:, rows], 0, gla0, h0, conv0, p, w, decode=False))
    y_p = jnp.concatenate([o[0] for o in parts], axis=0)
    gla_p, h_p, conv_p = (jnp.concatenate([o[k] for o in parts], axis=1) for k in (1, 2, 3))
    y_s, gla_s, h_s, conv_s = _trunk(x_sample, slice(0, nb_s), mod[:, nb_p:], PAST_LEN, state_gla, state_rglru_h,
                                     state_rglru_conv, p, w, decode=True)
    return (y_p, y_s, gla_p, gla_s, h_p, h_s, conv_p, conv_s)
```
